```python
import math
import numpy as np
import jax
import jax.numpy as jnp
from jax import lax

D_MODEL = 1024
BATCH = 16
SEQ = 2048
DEPTH = 2
DEC_BATCH = 8
DEC_SEQ = 4096
PAST_LEN = 128

BRANCH_WIDTH = D_MODEL // 2
N_BRANCH = 3
N_DIR = 2
RMS_EPS = 1e-6
ATT_HEAD_DIM = 64
ATT_HEADS = BRANCH_WIDTH // ATT_HEAD_DIM
ATT_KV_HEADS = ATT_HEADS // 4
ATT_GROUP = ATT_HEADS // ATT_KV_HEADS
KV_WIDTH = ATT_KV_HEADS * ATT_HEAD_DIM
ROT_DIM = ATT_HEAD_DIM // 4
ROPE_THETA = 500000.0
WINDOW = 128
ATT_BLOCK = 128
MLSTM_HEADS = 4
MLSTM_HEAD_DIM = BRANCH_WIDTH // MLSTM_HEADS
MLSTM_CHUNK = 128
SSM_HEAD_DIM = 64
SSM_HEADS = BRANCH_WIDTH // SSM_HEAD_DIM
SSM_GROUPS = 2
SSM_HEADS_PER_GROUP = SSM_HEADS // SSM_GROUPS
SSM_STATE = 64
SSM_CONV = 5
SSM_CHUNK = 128
CONV_WIDTH = BRANCH_WIDTH + 2 * SSM_GROUPS * SSM_STATE
IN_SPLITS = (
    BRANCH_WIDTH, KV_WIDTH, KV_WIDTH, BRANCH_WIDTH,
    BRANCH_WIDTH, BRANCH_WIDTH, BRANCH_WIDTH, BRANCH_WIDTH,
    N_DIR * MLSTM_HEADS, N_DIR * MLSTM_HEADS, BRANCH_WIDTH,
    CONV_WIDTH, N_DIR * SSM_HEADS, BRANCH_WIDTH,
    N_BRANCH * D_MODEL,
)
IN_WIDTH = sum(IN_SPLITS)

kernel_name = "hybrid_bidir_attn_mlstm_ssd_encoder"


def rms_norm(x, g):
    xf = x.astype(jnp.float32)
    y = xf * lax.rsqrt(jnp.mean(xf * xf, axis=-1, keepdims=True) + RMS_EPS)
    return (y * g.astype(jnp.float32)).astype(x.dtype)


def partial_rope(x, pos):
    half = ROT_DIM // 2
    inv_freq = ROPE_THETA ** (-jnp.arange(half, dtype=jnp.float32) * 2.0 / ROT_DIM)
    ang = pos.astype(jnp.float32)[:, None] * inv_freq
    cos = jnp.cos(ang)[:, None, :].astype(x.dtype)
    sin = jnp.sin(ang)[:, None, :].astype(x.dtype)
    x1 = x[..., :half]
    x2 = x[..., half:ROT_DIM]
    return jnp.concatenate([x1 * cos - x2 * sin, x2 * cos + x1 * sin, x[..., ROT_DIM:]], axis=-1)


def window_attention(q, k, v, sink):
    bsz, s_len = q.shape[0], q.shape[1]
    w = ATT_BLOCK
    nb = s_len // w
    qb = q.reshape(bsz, nb, w, ATT_KV_HEADS, ATT_GROUP, ATT_HEAD_DIM)

    def key_windows(a):
        ap = jnp.pad(a, ((0, 0), (w, w), (0, 0), (0, 0))).reshape(bsz, nb + 2, w, ATT_KV_HEADS, ATT_HEAD_DIM)
        return jnp.concatenate([ap[:, :-2], ap[:, 1:-1], ap[:, 2:]], axis=2)

    kw = key_windows(k)
    vw = key_windows(v)
    s = jnp.einsum('bnqhgd,bnkhd->bnhgqk', qb, kw).astype(jnp.float32) * (ATT_HEAD_DIM ** -0.5)
    blk = jnp.arange(nb)[:, None] * w
    qpos = blk + jnp.arange(w)
    kpos = blk - w + jnp.arange(3 * w)
    valid = ((jnp.abs(qpos[:, :, None] - kpos[:, None, :]) <= WINDOW)
             & (kpos >= 0)[:, None, :] & (kpos < s_len)[:, None, :])
    s = jnp.where(valid[None, :, None, None], s, -jnp.inf)
    sk = sink.astype(jnp.float32).reshape(ATT_KV_HEADS, ATT_GROUP)[None, None, :, :, None, None]
    mx = jnp.maximum(jnp.max(s, axis=-1, keepdims=True), sk)
    p = jnp.exp(s - mx)
    p = p / (jnp.sum(p, axis=-1, keepdims=True) + jnp.exp(sk - mx))
    o = jnp.einsum('bnhgqk,bnkhd->bnqhgd', p.astype(v.dtype), vw)
    return o.reshape(bsz, s_len, ATT_HEADS * ATT_HEAD_DIM)


def mlstm_bidir(q, k, v, ig, fg):
    bsz, s_len, nh, dk = q.shape
    dv = v.shape[-1]
    L = MLSTM_CHUNK
    nc = s_len // L
    f32 = jnp.float32

    def both(a):
        return jnp.stack([a, jnp.flip(a, 1)])

    def per_dir(a):
        return jnp.stack([a[:, :, 0], jnp.flip(a[:, :, 1], 1)])

    def chunks(a):
        return a.reshape(N_DIR, bsz, nc, L, nh, -1).transpose(2, 0, 1, 4, 3, 5)

    qc = chunks(both(q.astype(f32)))
    kc = chunks(both(k.astype(f32))) * (dk ** -0.5)
    vc = chunks(both(v.astype(f32)))
    igc = chunks(per_dir(ig.astype(f32))[..., None])[..., 0]
    lfc = chunks(per_dir(jax.nn.log_sigmoid(fg.astype(f32)))[..., None])[..., 0]
    causal = jnp.tril(jnp.ones((L, L), dtype=bool))

    def step(carry, inp):
        c_st, n_st, m_st = carry
        qj, kj, vj, ij, lf = inp
        b = jnp.cumsum(lf, axis=-1)
        inter = b + m_st[..., None]
        log_d = jnp.where(causal, b[..., :, None] - b[..., None, :] + ij[..., None, :], -jnp.inf)
        m_t = jnp.maximum(inter, jnp.max(log_d, axis=-1))
        w_inter = jnp.exp(inter - m_t)
        sc = jnp.einsum('...td,...sd->...ts', qj, kj) * jnp.exp(log_d - m_t[..., None])
        num = jnp.einsum('...ts,...se->...te', sc, vj) + w_inter[..., None] * jnp.einsum('...td,...de->...te', qj, c_st)
        den = jnp.sum(sc, axis=-1) + w_inter * jnp.einsum('...td,...d->...t', qj, n_st)
        h = num / jnp.maximum(jnp.abs(den), jnp.exp(-m_t))[..., None]
        b_last = b[..., -1]
        g = b_last[..., None] - b + ij
        m_new = jnp.maximum(b_last + m_st, jnp.max(g, axis=-1))
        wk = jnp.exp(g - m_new[..., None])
        decay = jnp.exp(b_last + m_st - m_new)
        c_new = decay[..., None, None] * c_st + jnp.einsum('...sd,...se,...s->...de', kj, vj, wk)
        n_new = decay[..., None] * n_st + jnp.einsum('...sd,...s->...d', kj, wk)
        return (c_new, n_new, m_new), h

    init = (jnp.zeros((N_DIR, bsz, nh, dk, dv), f32),
            jnp.zeros((N_DIR, bsz, nh, dk), f32),
            jnp.zeros((N_DIR, bsz, nh), f32))
    _, h = lax.scan(step, init, (qc, kc, vc, igc, lfc))
    h = h.transpose(1, 2, 0, 4, 3, 5).reshape(N_DIR, bsz, s_len, nh, dv)
    return h[0] + jnp.flip(h[1], 1)


def centred_depthwise_conv(x, w, b):
    pad = SSM_CONV // 2
    y = lax.conv_general_dilated(x, w[:, None, :].astype(x.dtype), window_strides=(1,),
                                 padding=[(pad, pad)], dimension_numbers=('NWC', 'WIO', 'NWC'),
                                 feature_group_count=x.shape[-1])
    return y + b.astype(x.dtype)


def ssd_bidir(xs, bm, cm, dt, a, d_skip):
    bsz, s_len = xs.shape[0], xs.shape[1]
    L = SSM_CHUNK
    nc = s_len // L
    G, E, P, N = SSM_GROUPS, SSM_HEADS_PER_GROUP, SSM_HEAD_DIM, SSM_STATE
    Z = N_DIR * bsz
    f32 = jnp.float32

    def both(arr):
        return jnp.stack([arr, jnp.flip(arr, 1)])

    x2 = both(xs.astype(f32)).reshape(Z, nc, L, G, E, P)
    b2 = both(bm.astype(f32)).reshape(Z, nc, L, G, N)
    c2 = both(cm.astype(f32)).reshape(Z, nc, L, G, N)
    dtd = jnp.stack([dt[:, :, 0], jnp.flip(dt[:, :, 1], 1)])
    da = (dtd * a[:, None, None, :]).reshape(Z, nc, L, G, E)
    xdt = x2 * dtd.reshape(Z, nc, L, G, E)[..., None]
    acs = jnp.cumsum(da, axis=2)
    causal = jnp.tril(jnp.ones((L, L), dtype=bool))[:, :, None, None]
    lmat = jnp.exp(jnp.where(causal, acs[:, :, :, None] - acs[:, :, None, :], -jnp.inf))
    cb = jnp.einsum('zclgn,zcsgn->zclsg', c2, b2)
    y_diag = jnp.einsum('zclsg,zclsge,zcsgep->zclgep', cb, lmat, xdt)
    decay_states = jnp.exp(acs[:, :, -1:] - acs)
    states = jnp.einsum('zclgn,zclge,zclgep->zcgepn', b2, decay_states, xdt)
    chunk_decay = jnp.exp(acs[:, :, -1])

    def step(h, inp):
        st, dec = inp
        return dec[..., None, None] * h + st, h

    _, prev = lax.scan(step, jnp.zeros((Z, G, E, P, N), f32),
                       (states.transpose(1, 0, 2, 3, 4, 5), chunk_decay.transpose(1, 0, 2, 3)))
    prev = prev.transpose(1, 0, 2, 3, 4, 5)
    y_off = jnp.einsum('zclgn,zcgepn,zclge->zclgep', c2, prev, jnp.exp(acs))
    y = (y_diag + y_off).reshape(N_DIR, bsz, s_len, SSM_HEADS, P)
    return y[0] + jnp.flip(y[1], 1) + d_skip.astype(f32)[:, None] * xs.astype(f32)


def encoder_layer(x, norm_g, w_in, q_norm_g, k_norm_g, attn_sink, w_att_out,
                  mlstm_i_b, mlstm_f_b, mlstm_norm_g, w_mlstm_out,
                  conv_w, conv_b, a_log, dt_bias, d_skip, ssm_norm_g, w_ssm_out, w_out):
    bsz, s_len, _ = x.shape
    h = rms_norm(x, norm_g)
    proj = h @ w_in
    bounds = np.cumsum(IN_SPLITS)[:-1].tolist()
    (aq, ak, av, az, mq, mk, mv, mo, mi, mf, mz, sxbc, sdt, sz, gates) = jnp.split(proj, bounds, axis=-1)
    pos = jnp.arange(s_len)

    aq = partial_rope(rms_norm(aq.reshape(bsz, s_len, ATT_HEADS, ATT_HEAD_DIM), q_norm_g), pos)
    ak = partial_rope(rms_norm(ak.reshape(bsz, s_len, ATT_KV_HEADS, ATT_HEAD_DIM), k_norm_g), pos)
    av = av.reshape(bsz, s_len, ATT_KV_HEADS, ATT_HEAD_DIM)
    ya = window_attention(aq, ak, av, attn_sink) * jax.nn.silu(az)

    hm = mlstm_bidir(mq.reshape(bsz, s_len, MLSTM_HEADS, MLSTM_HEAD_DIM),
                     mk.reshape(bsz, s_len, MLSTM_HEADS, MLSTM_HEAD_DIM),
                     mv.reshape(bsz, s_len, MLSTM_HEADS, MLSTM_HEAD_DIM),
                     mi.reshape(bsz, s_len, N_DIR, MLSTM_HEADS) + mlstm_i_b,
                     mf.reshape(bsz, s_len, N_DIR, MLSTM_HEADS) + mlstm_f_b)
    hm = rms_norm(hm, mlstm_norm_g.reshape(MLSTM_HEADS, MLSTM_HEAD_DIM)).reshape(bsz, s_len, BRANCH_WIDTH).astype(x.dtype)
    yb = hm * jax.nn.sigmoid(mo) * jax.nn.silu(mz)

    xbc = jax.nn.silu(centred_depthwise_conv(sxbc, conv_w, conv_b))
    sx, sbm, scm = jnp.split(xbc, [BRANCH_WIDTH, BRANCH_WIDTH + SSM_GROUPS * SSM_STATE], axis=-1)
    dt = jax.nn.softplus(sdt.reshape(bsz, s_len, N_DIR, SSM_HEADS).astype(jnp.float32) + dt_bias.astype(jnp.float32))
    a = -jnp.exp(a_log.astype(jnp.float32))
    ys = ssd_bidir(sx.reshape(bsz, s_len, SSM_HEADS, SSM_HEAD_DIM),
                   sbm.reshape(bsz, s_len, SSM_GROUPS, SSM_STATE),
                   scm.reshape(bsz, s_len, SSM_GROUPS, SSM_STATE), dt, a, d_skip)
    ys = ys.reshape(bsz, s_len, BRANCH_WIDTH).astype(x.dtype)
    yc = rms_norm(ys * jax.nn.silu(sz), ssm_norm_g)

    ga, gb, gc = jnp.split(jax.nn.sigmoid(gates), N_BRANCH, axis=-1)
    merged = ga * (ya @ w_att_out) + gb * (yb @ w_mlstm_out) + gc * (yc @ w_ssm_out)
    return x + merged @ w_out


def setup_inputs(seed: int = 0) -> dict:
    key = jax.random.key(seed)
    ks = jax.random.split(key, 24)
    f32 = jnp.float32

    def nrm(k, shape, scale):
        return jax.random.normal(k, shape, f32) * scale

    x_prompt = jax.random.normal(ks[0], (BATCH, SEQ, D_MODEL), f32)
    x_sample = jax.random.normal(ks[1], (DEC_BATCH, DEC_SEQ, D_MODEL), f32)
    norm_g = 1.0 + nrm(ks[2], (DEPTH, D_MODEL), 0.02)
    w_in = nrm(ks[3], (DEPTH, D_MODEL, IN_WIDTH), D_MODEL ** -0.5)
    q_norm_g = 1.0 + nrm(ks[4], (DEPTH, ATT_HEAD_DIM), 0.02)
    k_norm_g = 1.0 + nrm(ks[5], (DEPTH, ATT_HEAD_DIM), 0.02)
    attn_sink = nrm(ks[6], (DEPTH, ATT_HEADS), 0.5)
    w_att_out = nrm(ks[7], (DEPTH, BRANCH_WIDTH, D_MODEL), BRANCH_WIDTH ** -0.5)
    mlstm_i_b = nrm(ks[8], (DEPTH, N_DIR, MLSTM_HEADS), 0.1)
    mlstm_f_b = jnp.linspace(3.0, 6.0, MLSTM_HEADS, dtype=f32) + nrm(ks[9], (DEPTH, N_DIR, MLSTM_HEADS), 0.1)
    mlstm_norm_g = 1.0 + nrm(ks[10], (DEPTH, BRANCH_WIDTH), 0.02)
    w_mlstm_out = nrm(ks[11], (DEPTH, BRANCH_WIDTH, D_MODEL), BRANCH_WIDTH ** -0.5)
    conv_w = nrm(ks[12], (DEPTH, SSM_CONV, CONV_WIDTH), SSM_CONV ** -0.5)
    conv_b = nrm(ks[13], (DEPTH, CONV_WIDTH), 0.02)
    a_log = jnp.log(jax.random.uniform(ks[14], (DEPTH, N_DIR, SSM_HEADS), f32, 1.0, 16.0))
    u = jax.random.uniform(ks[15], (DEPTH, N_DIR, SSM_HEADS), f32)
    dt0 = jnp.exp(u * (math.log(0.1) - math.log(0.001)) + math.log(0.001))
    dt_bias = dt0 + jnp.log(-jnp.expm1(-dt0))
    d_skip = 1.0 + nrm(ks[16], (DEPTH, SSM_HEADS), 0.1)
    ssm_norm_g = 1.0 + nrm(ks[17], (DEPTH, BRANCH_WIDTH), 0.02)
    w_ssm_out = nrm(ks[18], (DEPTH, BRANCH_WIDTH, D_MODEL), BRANCH_WIDTH ** -0.5)
    w_out = nrm(ks[19], (DEPTH, D_MODEL, D_MODEL), D_MODEL ** -0.5)
    return {"x_prompt": x_prompt, "x_sample": x_sample, "norm_g": norm_g, "w_in": w_in,
            "q_norm_g": q_norm_g, "k_norm_g": k_norm_g, "attn_sink": attn_sink, "w_att_out": w_att_out,
            "mlstm_i_b": mlstm_i_b, "mlstm_f_b": mlstm_f_b, "mlstm_norm_g": mlstm_norm_g,
            "w_mlstm_out": w_mlstm_out, "conv_w": conv_w, "conv_b": conv_b, "a_log": a_log,
            "dt_bias": dt_bias, "d_skip": d_skip, "ssm_norm_g": ssm_norm_g, "w_ssm_out": w_ssm_out,
            "w_out": w_out}


def reference(x_prompt, x_sample, norm_g, w_in, q_norm_g, k_norm_g, attn_sink, w_att_out,
              mlstm_i_b, mlstm_f_b, mlstm_norm_g, w_mlstm_out, conv_w, conv_b, a_log,
              dt_bias, d_skip, ssm_norm_g, w_ssm_out, w_out):
    y_prompt = x_prompt
    y_sample = x_sample
    for layer in range(DEPTH):
        params = (norm_g[layer], w_in[layer], q_norm_g[layer], k_norm_g[layer], attn_sink[layer],
                  w_att_out[layer], mlstm_i_b[layer], mlstm_f_b[layer], mlstm_norm_g[layer],
                  w_mlstm_out[layer], conv_w[layer], conv_b[layer], a_log[layer], dt_bias[layer],
                  d_skip[layer], ssm_norm_g[layer], w_ssm_out[layer], w_out[layer])
        y_prompt = encoder_layer(y_prompt, *params)
        y_sample = encoder_layer(y_sample, *params)
    return (y_prompt, y_sample)
```

```python
import functools
import math

import jax
import jax.numpy as jnp
import numpy as np
from jax import lax
from jax.experimental import pallas as pl
from jax.experimental.pallas import tpu as pltpu

F32 = jnp.float32
BF16 = jnp.bfloat16

D_MODEL = 1024
BRANCH = 512
RMS_EPS = 1e-6
CHUNK = 128
LANES = 128
ATT_HEAD_DIM = 64
ROT_HALF = 8
ROPE_THETA = 500000.0
MLSTM_HEADS = 4
MLSTM_DK = 128
SSM_HEADS = 8
SSM_CONV = 5
HALO = 8
NEG = -1e30

P1_MQKV = 0
P1_AQKV = 1536
P1_AKV = 2048
P1_XBC = 2304
P1_GATES = 3072
P1_WIDTH = 3328
W2_WIDTH = 5120

VMEM_LIMIT = 56 * 1024 * 1024


def _dot(a, b):
    return jnp.dot(a, b, preferred_element_type=F32)


def _dot_nt(a, b):
    return lax.dot_general(a, b, (((1,), (1,)), ((), ())), preferred_element_type=F32)


def _split3(x):
    h1 = x.astype(BF16)
    r1 = x - h1.astype(F32)
    h2 = r1.astype(BF16)
    r2 = r1 - h2.astype(F32)
    return h1, h2, r2.astype(BF16)


def _cum_left(mask_bf, x):
    h1, h2, h3 = _split3(x)
    return _dot(mask_bf, h1) + _dot(mask_bf, h2) + _dot(mask_bf, h3)


def _cum_right(x, mask_bf):
    h1, h2, h3 = _split3(x)
    return _dot(h1, mask_bf) + _dot(h2, mask_bf) + _dot(h3, mask_bf)


def _sigmoid(x):
    return 1.0 / (1.0 + jnp.exp(-x))


def _silu(x):
    return x * _sigmoid(x)


def _softplus(x):
    return jnp.maximum(x, 0.0) + jnp.log1p(jnp.exp(-jnp.abs(x)))


def _log_sigmoid(x):
    return -_softplus(-x)


def _rms(x, g_row):
    ms = jnp.mean(x * x, axis=-1, keepdims=True)
    return x * lax.rsqrt(ms + RMS_EPS) * g_row


def _proj_kernel(x_ref, ng_ref, w_ref, o_ref):
    h = _rms(x_ref[...], ng_ref[...]).astype(BF16)
    step = 512
    for j0 in range(0, P1_WIDTH, step):
        j1 = min(j0 + step, P1_WIDTH)
        o_ref[:, j0:j1] = _dot(h, w_ref[:, j0:j1])


def _proj_call(x2, ng, w1, tm):
    t = x2.shape[0]
    return pl.pallas_call(
        _proj_kernel,
        grid=(t // tm,),
        in_specs=[
            pl.BlockSpec((tm, D_MODEL), lambda i: (i, 0)),
            pl.BlockSpec((1, D_MODEL), lambda i: (0, 0)),
            pl.BlockSpec((D_MODEL, P1_WIDTH), lambda i: (0, 0)),
        ],
        out_specs=pl.BlockSpec((tm, P1_WIDTH), lambda i: (i, 0)),
        out_shape=jax.ShapeDtypeStruct((t, P1_WIDTH), F32),
        compiler_params=pltpu.CompilerParams(
            dimension_semantics=("arbitrary",), vmem_limit_bytes=VMEM_LIMIT),
        name="proj",
    )(x2, ng, w1)


def _rope(x, tab):
    w = x.shape[1]
    return (x * tab[:, :w]
            + pltpu.roll(x, w - ROT_HALF, 1) * tab[:, w:2 * w]
            + pltpu.roll(x, ROT_HALF, 1) * tab[:, 2 * w:])


def _att_kernel(qkv_ref, kvp_ref, kvn_ref, rc_ref, rp_ref, rn_ref, qg_ref, kg_ref,
                bdq_ref, bdk_ref, sink_ref, o_ref):
    i = pl.program_id(1)
    nb = pl.num_programs(1)
    qkv = qkv_ref[...]
    kvp = kvp_ref[...]
    kvn = kvn_ref[...]
    q = qkv[:, :BRANCH]
    k = jnp.concatenate([kvp[:, :LANES], qkv[:, BRANCH:BRANCH + LANES], kvn[:, :LANES]], axis=0)
    v = jnp.concatenate([kvp[:, LANES:], qkv[:, BRANCH + LANES:], kvn[:, LANES:]], axis=0)
    tab_c = rc_ref[...]
    tab_k = jnp.concatenate([rp_ref[...], tab_c, rn_ref[...]], axis=0)
    tab_q = jnp.concatenate(
        [jnp.concatenate([tab_c[:, j * LANES:(j + 1) * LANES]] * 4, axis=1) for j in range(3)], axis=1)

    kms = _dot((k * k).astype(BF16), bdk_ref[...])
    kr = _rope(k * lax.rsqrt(kms + RMS_EPS) * kg_ref[...], tab_k)
    qms = _dot((q * q).astype(BF16), bdq_ref[...])
    qr = _rope(q * lax.rsqrt(qms + RMS_EPS) * qg_ref[...], tab_q)
    qr = (qr * (ATT_HEAD_DIM ** -0.5)).astype(BF16)

    lane = lax.broadcasted_iota(jnp.int32, (3 * CHUNK, LANES), 1)
    lo = lane < ATT_HEAD_DIM
    k_sw = pltpu.roll(kr, ATT_HEAD_DIM, 1)
    v_sw = pltpu.roll(v, ATT_HEAD_DIM, 1)
    zero = jnp.zeros_like(kr)
    k_var = [[jnp.where(lo, kr, zero).astype(BF16), jnp.where(lo, zero, k_sw).astype(BF16)],
             [jnp.where(lo, k_sw, zero).astype(BF16), jnp.where(lo, zero, kr).astype(BF16)]]
    v_var = [[jnp.where(lo, v, zero).astype(BF16), jnp.where(lo, zero, v_sw).astype(BF16)],
             [jnp.where(lo, v_sw, zero).astype(BF16), jnp.where(lo, zero, v).astype(BF16)]]

    r = lax.broadcasted_iota(jnp.int32, (CHUNK, 3 * CHUNK), 0)
    c = lax.broadcasted_iota(jnp.int32, (CHUNK, 3 * CHUNK), 1)
    valid = ((c >= r) & (c <= r + 2 * CHUNK)
             & ((c >= CHUNK) | (i > 0)) & ((c < 2 * CHUNK) | (i < nb - 1)))
    bias = jnp.where(valid, 0.0, NEG)

    outs = []
    for j in range(4):
        g = j // 2
        qp = qr[:, j * LANES:(j + 1) * LANES]
        acc = None
        for half in range(2):
            sink = sink_ref[2 * j + half]
            s = _dot_nt(qp, k_var[g][half]) + bias
            mx = jnp.maximum(jnp.max(s, axis=-1, keepdims=True), sink)
            p = jnp.exp(s - mx)
            den = jnp.sum(p, axis=-1, keepdims=True) + jnp.exp(sink - mx)
            o = _dot(p.astype(BF16), v_var[g][half]) / den
            acc = o if acc is None else acc + o
        outs.append(acc)
    o_ref[...] = jnp.concatenate(outs, axis=1)


def _att_call(p1, rope_tab, qg, kg, bdq, bdk, sink):
    b, s, _ = p1.shape
    nb = s // CHUNK
    return pl.pallas_call(
        _att_kernel,
        grid=(b, nb),
        in_specs=[
            pl.BlockSpec((None, CHUNK, 768), lambda bi, i: (bi, i, P1_AQKV // 768)),
            pl.BlockSpec((None, CHUNK, 256), lambda bi, i: (bi, jnp.maximum(i - 1, 0), P1_AKV // 256)),
            pl.BlockSpec((None, CHUNK, 256), lambda bi, i: (bi, jnp.minimum(i + 1, nb - 1), P1_AKV // 256)),
            pl.BlockSpec((CHUNK, 384), lambda bi, i: (i, 0)),
            pl.BlockSpec((CHUNK, 384), lambda bi, i: (jnp.maximum(i - 1, 0), 0)),
            pl.BlockSpec((CHUNK, 384), lambda bi, i: (jnp.minimum(i + 1, nb - 1), 0)),
            pl.BlockSpec((1, BRANCH), lambda bi, i: (0, 0)),
            pl.BlockSpec((1, LANES), lambda bi, i: (0, 0)),
            pl.BlockSpec((BRANCH, BRANCH), lambda bi, i: (0, 0)),
            pl.BlockSpec((LANES, LANES), lambda bi, i: (0, 0)),
            pl.BlockSpec(memory_space=pltpu.SMEM),
        ],
        out_specs=pl.BlockSpec((None, CHUNK, BRANCH), lambda bi, i: (bi, i, 0)),
        out_shape=jax.ShapeDtypeStruct((b, s, BRANCH), F32),
        compiler_params=pltpu.CompilerParams(
            dimension_semantics=("arbitrary", "arbitrary"), vmem_limit_bytes=VMEM_LIMIT),
        name="att",
    )(p1, p1, p1, rope_tab, rope_tab, rope_tab, qg, kg, bdq, bdk, sink)


def _chunk_index(d, c, nc):
    return c + d * (nc - 1 - 2 * c)


def _scan_masks(d):
    r = lax.broadcasted_iota(jnp.int32, (CHUNK, CHUNK), 0)
    c = lax.broadcasted_iota(jnp.int32, (CHUNK, CHUNK), 1)
    diff = (r - c) * (1 - 2 * d)
    causal = diff >= 0
    anti = diff <= 0
    return causal, anti


def _mlstm_kernel(qkv_ref, g_ref, dp_ref, ng_ref, o_ref, c_sc, n_sc, m_sc):
    d = pl.program_id(1)
    c = pl.program_id(2)
    nc = pl.num_programs(2)
    rows = pl.ds(pl.multiple_of(_chunk_index(d, c, nc) * CHUNK, CHUNK), CHUNK)

    @pl.when(c == 0)
    def _():
        c_sc[...] = jnp.zeros_like(c_sc)
        n_sc[...] = jnp.zeros_like(n_sc)
        m_sc[...] = jnp.zeros_like(m_sc)

    causal, anti = _scan_masks(d)
    g = g_ref[...] + dp_ref[0:1, :]
    gt = g.T
    lf_c = _log_sigmoid(g)
    lf_r = _log_sigmoid(gt)
    b_c = _cum_left(causal.astype(BF16), lf_c)
    b_r = _cum_right(lf_r, anti.astype(BF16))
    scale = MLSTM_DK ** -0.5

    for h in range(MLSTM_HEADS):
        fl = MLSTM_HEADS + h
        ig_c = g[:, h:h + 1]
        ig_r = gt[h:h + 1, :]
        bc = b_c[:, fl:fl + 1]
        br = b_r[fl:fl + 1, :]
        btot = jnp.sum(lf_r[fl:fl + 1, :], axis=1, keepdims=True)
        m_st = m_sc[h:h + 1, 0:1]
        q = qkv_ref[:, h * MLSTM_DK:(h + 1) * MLSTM_DK]
        k = qkv_ref[:, BRANCH + h * MLSTM_DK:BRANCH + (h + 1) * MLSTM_DK]
        v = qkv_ref[:, 2 * BRANCH + h * MLSTM_DK:2 * BRANCH + (h + 1) * MLSTM_DK]
        qb = q.astype(BF16)
        vb = v.astype(BF16)

        inter = bc + m_st
        log_d = jnp.where(causal, bc - br + ig_r, NEG)
        m_t = jnp.maximum(inter, jnp.max(log_d, axis=1, keepdims=True))
        w_inter = jnp.exp(inter - m_t)
        sc = _dot_nt(qb, k.astype(BF16)) * (jnp.exp(log_d - m_t) * scale)
        c_st = c_sc[h]
        n_st = n_sc[h:h + 1, :]
        num = _dot(sc.astype(BF16), vb) + w_inter * _dot(qb, c_st.astype(BF16))
        den = jnp.sum(sc, axis=1, keepdims=True) + w_inter * jnp.sum(q * n_st, axis=1, keepdims=True)
        hh = num / jnp.maximum(jnp.abs(den), jnp.exp(-m_t))

        m_new = jnp.maximum(btot + m_st, jnp.max(btot - br + ig_r, axis=1, keepdims=True))
        wk = jnp.exp(btot - bc + ig_c - m_new)
        decay = jnp.exp(btot + m_st - m_new)
        kw = k * (wk * scale)
        c_sc[h] = decay * c_st + _dot(kw.T.astype(BF16), vb)
        n_sc[h:h + 1, :] = decay * n_st + jnp.sum(kw, axis=0, keepdims=True)
        m_sc[h:h + 1, :] = jnp.broadcast_to(m_new, (1, LANES))

        cols = slice(h * MLSTM_DK, (h + 1) * MLSTM_DK)

        @pl.when(d == 0)
        def _():
            o_ref[rows, cols] = hh

        @pl.when(d == 1)
        def _():
            o_ref[rows, cols] = _rms(o_ref[rows, cols] + hh, ng_ref[:, cols])


def _mlstm_call(p1, dirp, ng):
    b, s, _ = p1.shape
    nc = s // CHUNK
    return pl.pallas_call(
        _mlstm_kernel,
        grid=(b, 2, nc),
        in_specs=[
            pl.BlockSpec((None, CHUNK, 3 * BRANCH), lambda bi, d, c: (bi, _chunk_index(d, c, nc), 0)),
            pl.BlockSpec((None, CHUNK, LANES),
                         lambda bi, d, c: (bi, _chunk_index(d, c, nc), P1_GATES // LANES + d)),
            pl.BlockSpec((None, 8, LANES), lambda bi, d, c: (d, 0, 0)),
            pl.BlockSpec((1, BRANCH), lambda bi, d, c: (0, 0)),
        ],
        out_specs=pl.BlockSpec((None, s, BRANCH), lambda bi, d, c: (bi, 0, 0)),
        out_shape=jax.ShapeDtypeStruct((b, s, BRANCH), F32),
        scratch_shapes=[
            pltpu.VMEM((MLSTM_HEADS, MLSTM_DK, MLSTM_DK), F32),
            pltpu.VMEM((8, LANES), F32),
            pltpu.VMEM((8, LANES), F32),
        ],
        compiler_params=pltpu.CompilerParams(
            dimension_semantics=("arbitrary", "arbitrary", "arbitrary"), vmem_limit_bytes=VMEM_LIMIT),
        name="mlstm",
    )(p1, p1, dirp, ng)


def _ssd_kernel(x_ref, xp_ref, xn_ref, g_ref, dp_ref, cw_ref, cb_ref, dsk_ref, o_ref, st_sc, pad_sc):
    d = pl.program_id(1)
    c = pl.program_id(2)
    nc = pl.num_programs(2)
    cidx = _chunk_index(d, c, nc)
    rows = pl.ds(pl.multiple_of(cidx * CHUNK, CHUNK), CHUNK)

    @pl.when(c == 0)
    def _():
        st_sc[...] = jnp.zeros_like(st_sc)

    pad_sc[0:HALO, :] = jnp.where(cidx > 0, xp_ref[...], 0.0)
    pad_sc[HALO:HALO + CHUNK, :] = x_ref[...]
    pad_sc[HALO + CHUNK:, :] = jnp.where(cidx < nc - 1, xn_ref[...], 0.0)
    acc = jnp.broadcast_to(cb_ref[...], (CHUNK, 768))
    for t in range(SSM_CONV):
        off = HALO - SSM_CONV // 2 + t
        acc = acc + pad_sc[off:off + CHUNK, :] * cw_ref[t:t + 1, :]
    xc = _silu(acc)
    xs = xc[:, :BRANCH]
    bt = xc[:, BRANCH:BRANCH + LANES].T
    cm = xc[:, BRANCH + LANES:]

    causal, anti = _scan_masks(d)
    dt = _softplus(g_ref[...] + dp_ref[0:1, :])
    da = dt * (-jnp.exp(dp_ref[1:2, :]))
    dt_t = dt.T
    da_t = da.T
    acs_c = _cum_left(causal.astype(BF16), da)
    acs_r = _cum_right(da_t, anti.astype(BF16))
    tot = jnp.sum(da_t, axis=1, keepdims=True)

    lane = lax.broadcasted_iota(jnp.int32, (CHUNK, LANES), 1)
    row = lax.broadcasted_iota(jnp.int32, (CHUNK, LANES), 0)
    lo = lane < 64
    lo_row = lo[0:1, :]
    bt_b = bt.astype(BF16)
    cm_b = cm.astype(BF16)
    zero = jnp.zeros((CHUNK, LANES), F32)
    cb = [_dot(jnp.where(lo, cm, zero).astype(BF16), bt_b),
          _dot(jnp.where(lo, zero, cm).astype(BF16), bt_b)]

    ys = []
    for j in range(4):
        grp = j // 2
        m_parts, bw_parts, eac, edec = [], [], [], []
        for e in (2 * j, 2 * j + 1):
            ln = SSM_HEADS + e
            ac = acs_c[:, ln:ln + 1]
            ar = acs_r[ln:ln + 1, :]
            dtr = dt_t[ln:ln + 1, :]
            te = tot[ln:ln + 1, :]
            lm = jnp.exp(jnp.where(causal, ac - ar, NEG))
            m_parts.append((cb[grp] * lm * dtr).astype(BF16))
            bw_parts.append((bt * (jnp.exp(te - ar) * dtr)).astype(BF16))
            eac.append(jnp.exp(ac))
            edec.append(jnp.exp(te))
        lhs = jnp.concatenate([jnp.concatenate(m_parts, axis=1),
                               jnp.concatenate(bw_parts, axis=1)], axis=0)
        xp = xs[:, j * LANES:(j + 1) * LANES]
        x2 = jnp.concatenate([jnp.where(lo, xp, zero), jnp.where(lo, zero, xp)], axis=0).astype(BF16)
        res = _dot(lhs, x2)
        in_grp = (row >= 64 * grp) & (row < 64 * grp + 64)
        s_new = jnp.where(in_grp, res[CHUNK:], zero)
        prev = st_sc[j]
        y_off = _dot(cm_b, prev.astype(BF16)) * jnp.where(lo, eac[0], eac[1])
        ys.append(res[:CHUNK] + y_off)
        st_sc[j] = prev * jnp.where(lo_row, edec[0], edec[1]) + s_new
    y = jnp.concatenate(ys, axis=1)

    @pl.when(d == 0)
    def _():
        o_ref[rows, :] = y + dsk_ref[...] * xs

    @pl.when(d == 1)
    def _():
        o_ref[rows, :] = o_ref[rows, :] + y


def _ssd_call(p1, dirp, conv_w8, conv_b, dskip):
    b, s, _ = p1.shape
    nc = s // CHUNK
    per = CHUNK // HALO
    last = s // HALO - 1

    def cur(bi, d, c):
        return (bi, _chunk_index(d, c, nc), P1_XBC // 768)

    def prev(bi, d, c):
        return (bi, jnp.maximum(_chunk_index(d, c, nc) * per - 1, 0), P1_XBC // 768)

    def nxt(bi, d, c):
        return (bi, jnp.minimum(_chunk_index(d, c, nc) * per + per, last), P1_XBC // 768)

    return pl.pallas_call(
        _ssd_kernel,
        grid=(b, 2, nc),
        in_specs=[
            pl.BlockSpec((None, CHUNK, 768), cur),
            pl.BlockSpec((None, HALO, 768), prev),
            pl.BlockSpec((None, HALO, 768), nxt),
            pl.BlockSpec((None, CHUNK, LANES),
                         lambda bi, d, c: (bi, _chunk_index(d, c, nc), P1_GATES // LANES + d)),
            pl.BlockSpec((None, 8, LANES), lambda bi, d, c: (d, 0, 0)),
            pl.BlockSpec((8, 768), lambda bi, d, c: (0, 0)),
            pl.BlockSpec((1, 768), lambda bi, d, c: (0, 0)),
            pl.BlockSpec((1, BRANCH), lambda bi, d, c: (0, 0)),
        ],
        out_specs=pl.BlockSpec((None, s, BRANCH), lambda bi, d, c: (bi, 0, 0)),
        out_shape=jax.ShapeDtypeStruct((b, s, BRANCH), F32),
        scratch_shapes=[
            pltpu.VMEM((4, CHUNK, LANES), F32),
            pltpu.VMEM((CHUNK + 2 * HALO, 768), F32),
        ],
        compiler_params=pltpu.CompilerParams(
            dimension_semantics=("arbitrary", "arbitrary", "arbitrary"), vmem_limit_bytes=VMEM_LIMIT),
        name="ssd",
    )(p1, p1, p1, p1, dirp, conv_w8, conv_b, dskip)


def _final_kernel(x_ref, ya_ref, hm_ref, ys_ref, ng_ref, sg_ref, w2_ref, wa_ref, wb_ref, wc_ref,
                  wo_ref, o_ref):
    x = x_ref[...]
    h = _rms(x, ng_ref[...]).astype(BF16)

    def proj(k):
        return _dot(h, w2_ref[:, k * BRANCH:(k + 1) * BRANCH])

    def gate(k):
        lo = 4 * BRANCH + k * D_MODEL
        return _sigmoid(_dot(h, w2_ref[:, lo:lo + D_MODEL]))

    ya = ya_ref[...] * _silu(proj(0))
    merged = gate(0) * _dot(ya.astype(BF16), wa_ref[...])
    yb = hm_ref[...] * _sigmoid(proj(1)) * _silu(proj(2))
    merged = merged + gate(1) * _dot(yb.astype(BF16), wb_ref[...])
    yc = _rms(ys_ref[...] * _silu(proj(3)), sg_ref[...])
    merged = merged + gate(2) * _dot(yc.astype(BF16), wc_ref[...])
    o_ref[...] = x + _dot(merged.astype(BF16), wo_ref[...])


def _final_call(x2, ya, hm, ys, ng, sg, w2, wa, wb, wc, wo, tm):
    t = x2.shape[0]
    row = lambda width: pl.BlockSpec((tm, width), lambda i: (i, 0))
    full = lambda shape: pl.BlockSpec(shape, lambda i: (0, 0))
    return pl.pallas_call(
        _final_kernel,
        grid=(t // tm,),
        in_specs=[
            row(D_MODEL), row(BRANCH), row(BRANCH), row(BRANCH),
            full((1, D_MODEL)), full((1, BRANCH)),
            full((D_MODEL, W2_WIDTH)), full((BRANCH, D_MODEL)), full((BRANCH, D_MODEL)),
            full((BRANCH, D_MODEL)), full((D_MODEL, D_MODEL)),
        ],
        out_specs=row(D_MODEL),
        out_shape=jax.ShapeDtypeStruct((t, D_MODEL), F32),
        compiler_params=pltpu.CompilerParams(
            dimension_semantics=("arbitrary",), vmem_limit_bytes=VMEM_LIMIT),
        name="final",
    )(x2, ya, hm, ys, ng, sg, w2, wa, wb, wc, wo)


_IN_SPLITS = (512, 128, 128, 512, 512, 512, 512, 512, 8, 8, 512, 768, 16, 512, 3072)
_NAMES = ("aq", "ak", "av", "az", "mq", "mk", "mv", "mo", "mi", "mf", "mz", "sxbc", "sdt", "sz", "gates")


def _split_w_in(w):
    bounds = np.cumsum((0,) + _IN_SPLITS)
    return {n: w[:, bounds[i]:bounds[i + 1]] for i, n in enumerate(_NAMES)}


def _layer_params(w_in, q_norm_g, k_norm_g, mlstm_i_b, mlstm_f_b, a_log, dt_bias, conv_w, conv_b, d_skip):
    p = _split_w_in(w_in)
    gate_cols = []
    for d in range(2):
        gate_cols += [p["mi"][:, 4 * d:4 * d + 4], p["mf"][:, 4 * d:4 * d + 4], p["sdt"][:, 8 * d:8 * d + 8],
                      jnp.zeros((D_MODEL, LANES - 16), F32)]
    w1 = jnp.concatenate([p["mq"], p["mk"], p["mv"], p["aq"], p["ak"], p["av"], p["sxbc"]] + gate_cols,
                         axis=1).astype(BF16)
    w2 = jnp.concatenate([p["az"], p["mo"], p["mz"], p["sz"], p["gates"]], axis=1).astype(BF16)
    dirp = jnp.zeros((2, 8, LANES), F32)
    dirp = dirp.at[:, 0, 0:4].set(mlstm_i_b).at[:, 0, 4:8].set(mlstm_f_b).at[:, 0, 8:16].set(dt_bias)
    dirp = dirp.at[:, 1, 8:16].set(a_log)
    return dict(
        w1=w1, w2=w2, dirp=dirp,
        qg=jnp.tile(q_norm_g, 8)[None, :], kg=jnp.tile(k_norm_g, 2)[None, :],
        conv_w8=jnp.zeros((8, 768), F32).at[:SSM_CONV].set(conv_w), conv_b=conv_b[None, :],
        dskip=jnp.repeat(d_skip, 64)[None, :])


def _rope_table(s_len):
    inv_freq = ROPE_THETA ** (-jnp.arange(ROT_HALF, dtype=F32) * 2.0 / (2 * ROT_HALF))
    ang = jnp.arange(s_len).astype(F32)[:, None] * inv_freq
    cos, sin = jnp.cos(ang), jnp.sin(ang)
    rest = ATT_HEAD_DIM - 2 * ROT_HALF
    one = jnp.ones((s_len, rest), F32)
    zer = jnp.zeros((s_len, rest), F32)
    z8 = jnp.zeros((s_len, ROT_HALF), F32)
    tc = jnp.concatenate([cos, cos, one], axis=1)
    t1 = jnp.concatenate([-sin, z8, zer], axis=1)
    t2 = jnp.concatenate([z8, sin, zer], axis=1)
    return jnp.concatenate([jnp.tile(t, (1, 2)) for t in (tc, t1, t2)], axis=1)


def _block_diag_mean(width):
    blk = np.kron(np.eye(width // ATT_HEAD_DIM), np.full((ATT_HEAD_DIM, ATT_HEAD_DIM), 1.0 / ATT_HEAD_DIM))
    return jnp.asarray(blk, BF16)


def _layer(x, lp, rope_tab, bdq, bdk, norm_g, attn_sink, mlstm_norm_g, ssm_norm_g, wa, wb, wc, wo):
    b, s, _ = x.shape
    assert s % CHUNK == 0 and s >= 2 * CHUNK and (b * s) % 512 == 0, (b, s)
    x2 = x.reshape(b * s, D_MODEL)
    p1 = _proj_call(x2, norm_g[None, :], lp["w1"], 512).reshape(b, s, P1_WIDTH)
    ya = _att_call(p1, rope_tab, lp["qg"], lp["kg"], bdq, bdk, attn_sink)
    hm = _mlstm_call(p1, lp["dirp"], mlstm_norm_g[None, :])
    ys = _ssd_call(p1, lp["dirp"], lp["conv_w8"], lp["conv_b"], lp["dskip"])
    out = _final_call(x2, ya.reshape(b * s, BRANCH), hm.reshape(b * s, BRANCH), ys.reshape(b * s, BRANCH),
                      norm_g[None, :], ssm_norm_g[None, :], lp["w2"], wa, wb, wc, wo, 256)
    return out.reshape(b, s, D_MODEL)


def kernel(x_prompt, x_sample, norm_g, w_in, q_norm_g, k_norm_g, attn_sink, w_att_out, mlstm_i_b, mlstm_f_b,
           mlstm_norm_g, w_mlstm_out, conv_w, conv_b, a_log, dt_bias, d_skip, ssm_norm_g, w_ssm_out, w_out):
    depth = w_in.shape[0]
    bdq = _block_diag_mean(BRANCH)
    bdk = _block_diag_mean(LANES)
    tabs = {x.shape[1]: _rope_table(x.shape[1]) for x in (x_prompt, x_sample)}
    ys = [x_prompt, x_sample]
    for l in range(depth):
        lp = _layer_params(w_in[l], q_norm_g[l], k_norm_g[l], mlstm_i_b[l], mlstm_f_b[l], a_log[l],
                           dt_bias[l], conv_w[l], conv_b[l], d_skip[l])
        wa, wb, wc, wo = (w.astype(BF16) for w in (w_att_out[l], w_mlstm_out[l], w_ssm_out[l], w_out[l]))
        ys = [_layer(y, lp, tabs[y.shape[1]], bdq, bdk, norm_g[l], attn_sink[l], mlstm_norm_g[l],
                     ssm_norm_g[l], wa, wb, wc, wo) for y in ys]
    return tuple(ys)
```

```python
import jax
import jax.numpy as jnp
import numpy as np
from jax import lax
from jax.experimental import pallas as pl
from jax.experimental.pallas import tpu as pltpu

F32 = jnp.float32
BF16 = jnp.bfloat16

D_MODEL = 1024
BRANCH = 512
RMS_EPS = 1e-6
CHUNK = 128
LANES = 128
SUBLANES = 8
ATT_HEADS = 8
ATT_HEAD_DIM = 64
ROT_HALF = 8
ROPE_THETA = 500000.0
MLSTM_HEADS = 4
MLSTM_DK = 128
SSM_HEADS = 8
SSM_CONV = 5
HALO = SUBLANES
NEG = -1e30

MQK_W = 1024
XBC_W = 768
SMALL_W = 384
ROW_W = MQK_W + XBC_W + SMALL_W
T_MV, T_AQ, T_AV, T_GATE = 0, 512, 1024, 1152
T_ROWS = 1184
W2_WIDTH = 5120
PROJ_TM = 512
FINAL_TM = 256

VMEM_LIMIT = 56 * 1024 * 1024


def _dot(a, b):
    return jnp.dot(a, b, preferred_element_type=F32)


def _dot_nt(a, b):
    return lax.dot_general(a, b, (((1,), (1,)), ((), ())), preferred_element_type=F32)


def _split3(x):
    h1 = x.astype(BF16)
    r1 = x - h1.astype(F32)
    h2 = r1.astype(BF16)
    r2 = r1 - h2.astype(F32)
    return h1, h2, r2.astype(BF16)


def _cum_left(mask_bf, x):
    h1, h2, h3 = _split3(x)
    return _dot(mask_bf, h1) + _dot(mask_bf, h2) + _dot(mask_bf, h3)


def _cum_right(x, mask_bf):
    h1, h2, h3 = _split3(x)
    return _dot(h1, mask_bf) + _dot(h2, mask_bf) + _dot(h3, mask_bf)


def _sigmoid(x):
    return 1.0 / (1.0 + jnp.exp(-x))


def _silu(x):
    return x * _sigmoid(x)


def _softplus(x):
    return jnp.maximum(x, 0.0) + jnp.log1p(jnp.exp(-jnp.abs(x)))


def _log_sigmoid(x):
    return -_softplus(-x)


def _rms(x, g_row):
    ms = jnp.mean(x * x, axis=-1, keepdims=True)
    return x * lax.rsqrt(ms + RMS_EPS) * g_row


def _proj_kernel(x_ref, ng_ref, wr_ref, wt_ref, mqk_ref, xbc_ref, small_ref, t_ref):
    h = _rms(x_ref[...], ng_ref[...]).astype(BF16)
    for j0 in range(0, MQK_W, 512):
        mqk_ref[:, j0:j0 + 512] = _dot(h, wr_ref[:, j0:j0 + 512])
    xbc_ref[...] = _dot(h, wr_ref[:, MQK_W:MQK_W + XBC_W])
    small_ref[...] = _dot(h, wr_ref[:, MQK_W + XBC_W:])
    for r0 in range(0, T_ROWS, 512):
        r1 = min(r0 + 512, T_ROWS)
        res = _dot_nt(wt_ref[r0:r1, :], h)
        for j in range(PROJ_TM // CHUNK):
            t_ref[j, r0:r1, :] = res[:, j * CHUNK:(j + 1) * CHUNK]


def _proj_call(x2, ng, w_row, w_t):
    t = x2.shape[0]
    tm = PROJ_TM
    full = lambda shape: pl.BlockSpec(shape, lambda i: (0, 0))
    return pl.pallas_call(
        _proj_kernel,
        grid=(t // tm,),
        in_specs=[
            pl.BlockSpec((tm, D_MODEL), lambda i: (i, 0)),
            full((1, D_MODEL)), full((D_MODEL, ROW_W)), full((T_ROWS, D_MODEL)),
        ],
        out_specs=[
            pl.BlockSpec((tm, MQK_W), lambda i: (i, 0)),
            pl.BlockSpec((tm, XBC_W), lambda i: (i, 0)),
            pl.BlockSpec((tm, SMALL_W), lambda i: (i, 0)),
            pl.BlockSpec((tm // CHUNK, T_ROWS, CHUNK), lambda i: (i, 0, 0)),
        ],
        out_shape=[
            jax.ShapeDtypeStruct((t, MQK_W), F32),
            jax.ShapeDtypeStruct((t, XBC_W), F32),
            jax.ShapeDtypeStruct((t, SMALL_W), F32),
            jax.ShapeDtypeStruct((t // CHUNK, T_ROWS, CHUNK), F32),
        ],
        compiler_params=pltpu.CompilerParams(
            dimension_semantics=("arbitrary",), vmem_limit_bytes=VMEM_LIMIT),
        name="proj",
    )(x2, ng, w_row, w_t)


def _rope_rows(x, tab):
    w = x.shape[1]
    return (x * tab[:, :w]
            + pltpu.roll(x, w - ROT_HALF, 1) * tab[:, w:2 * w]
            + pltpu.roll(x, ROT_HALF, 1) * tab[:, 2 * w:])


def _att_kernel(kp_ref, kc_ref, kn_ref, qt_ref, vp_ref, vc_ref, vn_ref, rp_ref, rc_ref, rn_ref,
                rt_ref, qg_ref, kg_ref, bdk_ref, sink_ref, o_ref):
    i = pl.program_id(1)
    nb = pl.num_programs(1)
    win = 3 * CHUNK

    k = jnp.concatenate([kp_ref[...], kc_ref[...], kn_ref[...]], axis=0)
    tab_k = jnp.concatenate([rp_ref[...], rc_ref[...], rn_ref[...]], axis=0)
    kms = _dot((k * k).astype(BF16), bdk_ref[...])
    kr = _rope_rows(k * lax.rsqrt(kms + RMS_EPS) * kg_ref[...], tab_k)
    lane = lax.broadcasted_iota(jnp.int32, (win, LANES), 1)
    lo = lane < ATT_HEAD_DIM
    k_sw = pltpu.roll(kr, ATT_HEAD_DIM, 1)
    zero = jnp.zeros_like(kr)
    k_var = [[jnp.where(lo, kr, zero).astype(BF16), jnp.where(lo, zero, k_sw).astype(BF16)],
             [jnp.where(lo, k_sw, zero).astype(BF16), jnp.where(lo, zero, kr).astype(BF16)]]

    v_t = jnp.concatenate([vp_ref[...], vc_ref[...], vn_ref[...]], axis=1).astype(BF16)

    cos = rt_ref[0:ROT_HALF, :]
    sin = rt_ref[ROT_HALF:, :]
    q_heads = []
    for h in range(ATT_HEADS):
        x = qt_ref[h * ATT_HEAD_DIM:(h + 1) * ATT_HEAD_DIM, :]
        ms = jnp.mean(x * x, axis=0, keepdims=True)
        xn = x * lax.rsqrt(ms + RMS_EPS) * qg_ref[...]
        x1 = xn[0:ROT_HALF]
        x2 = xn[ROT_HALF:2 * ROT_HALF]
        xr = jnp.concatenate([x1 * cos - x2 * sin, x2 * cos + x1 * sin, xn[2 * ROT_HALF:]], axis=0)
        q_heads.append((xr * (ATT_HEAD_DIM ** -0.5)).astype(BF16))

    s_idx = lax.broadcasted_iota(jnp.int32, (win, CHUNK), 0)
    t_idx = lax.broadcasted_iota(jnp.int32, (win, CHUNK), 1)
    valid = ((s_idx >= t_idx) & (s_idx <= t_idx + 2 * CHUNK)
             & ((s_idx >= CHUNK) | (i > 0)) & ((s_idx < 2 * CHUNK) | (i < nb - 1)))
    bias = jnp.where(valid, 0.0, NEG)

    outs = []
    for h in range(ATT_HEADS):
        g = h // 4
        half = h % 2
        q_pair = jnp.concatenate([q_heads[h - half], q_heads[h - half + 1]], axis=0)
        sink = sink_ref[h]
        s = _dot(k_var[g][half], q_pair) + bias
        mx = jnp.maximum(jnp.max(s, axis=0, keepdims=True), sink)
        p = jnp.exp(s - mx)
        den = jnp.sum(p, axis=0, keepdims=True) + jnp.exp(sink - mx)
        o = _dot(v_t[g * ATT_HEAD_DIM:(g + 1) * ATT_HEAD_DIM, :], p.astype(BF16))
        outs.append(o * (1.0 / den))
    o_ref[...] = jnp.concatenate(outs, axis=0)


def _att_call(small, p_t, b, s, rope_rows, rope_t, qg_t, kg, bdk, sink):
    nb = s // CHUNK
    small3 = small.reshape(b, s, SMALL_W)
    prev = lambda i: jnp.maximum(i - 1, 0)
    nxt = lambda i: jnp.minimum(i + 1, nb - 1)
    kspec = lambda f: pl.BlockSpec((None, CHUNK, LANES), lambda bi, i: (bi, f(i), 0))
    vspec = lambda f: pl.BlockSpec((None, LANES, CHUNK), lambda bi, i: (bi * nb + f(i), T_AV // LANES, 0))
    rspec = lambda f: pl.BlockSpec((CHUNK, 3 * LANES), lambda bi, i: (f(i), 0))
    same = lambda i: i
    return pl.pallas_call(
        _att_kernel,
        grid=(b, nb),
        in_specs=[
            kspec(prev), kspec(same), kspec(nxt),
            pl.BlockSpec((None, BRANCH, CHUNK), lambda bi, i: (bi * nb + i, T_AQ // BRANCH, 0)),
            vspec(prev), vspec(same), vspec(nxt),
            rspec(prev), rspec(same), rspec(nxt),
            pl.BlockSpec((None, 2 * ROT_HALF, CHUNK), lambda bi, i: (i, 0, 0)),
            pl.BlockSpec((ATT_HEAD_DIM, LANES), lambda bi, i: (0, 0)),
            pl.BlockSpec((1, LANES), lambda bi, i: (0, 0)),
            pl.BlockSpec((LANES, LANES), lambda bi, i: (0, 0)),
            pl.BlockSpec(memory_space=pltpu.SMEM),
        ],
        out_specs=pl.BlockSpec((None, BRANCH, CHUNK), lambda bi, i: (bi * nb + i, 0, 0)),
        out_shape=jax.ShapeDtypeStruct((b * nb, BRANCH, CHUNK), F32),
        compiler_params=pltpu.CompilerParams(
            dimension_semantics=("arbitrary", "arbitrary"), vmem_limit_bytes=VMEM_LIMIT),
        name="att",
    )(small3, small3, small3, p_t, p_t, p_t, p_t, rope_rows, rope_rows, rope_rows, rope_t,
      qg_t, kg, bdk, sink)


def _chunk_index(d, c, nc):
    return c + d * (nc - 1 - 2 * c)


def _scan_masks(d):
    r = lax.broadcasted_iota(jnp.int32, (CHUNK, CHUNK), 0)
    c = lax.broadcasted_iota(jnp.int32, (CHUNK, CHUNK), 1)
    diff = (r - c) * (1 - 2 * d)
    causal = diff >= 0
    anti = diff <= 0
    return causal, anti


AUG = MLSTM_DK + 16
LOG_QK_SCALE = float(np.log(MLSTM_DK ** -0.5))


def _mlstm_kernel(qk_ref, vt_ref, g_ref, dp_ref, ng_ref, o_ref, st_sc, m_sc):
    d = pl.program_id(1)
    c = pl.program_id(2)
    nc = pl.num_programs(2)
    cidx = _chunk_index(d, c, nc)

    @pl.when(c == 0)
    def _():
        st_sc[...] = jnp.zeros_like(st_sc)
        m_sc[...] = jnp.zeros_like(m_sc)

    _, anti = _scan_masks(d)
    ig = g_ref[0:SUBLANES, :] + dp_ref[0:SUBLANES, :]
    lf = _log_sigmoid(g_ref[SUBLANES:, :] + dp_ref[SUBLANES:, :])
    b = _cum_right(lf, anti.astype(BF16))
    a = ig - b
    m_st = m_sc[...][:, 0:1]
    mm = jnp.maximum(m_st, jnp.max(a, axis=1, keepdims=True))
    scale = MLSTM_DK ** -0.5
    wk = jnp.exp(a - mm) * scale
    decay = jnp.exp(m_st - mm)
    m_sc[...] = jnp.broadcast_to(jnp.sum(lf, axis=1, keepdims=True) + mm, (SUBLANES, LANES))
    a_col = jnp.concatenate([a, jnp.zeros((CHUNK - SUBLANES, LANES), F32)], axis=0).T
    ones = jnp.ones((AUG - MLSTM_DK, CHUNK), F32)

    outs = []
    for h in range(MLSTM_HEADS):
        a_m = jnp.where(anti, jnp.broadcast_to(a_col[:, h:h + 1], (CHUNK, CHUNK)), NEG)
        m_h = m_st[h:h + 1, :]
        u = jnp.maximum(m_h, jnp.max(a_m, axis=0, keepdims=True))
        m_t = b[h:h + 1, :] + u
        w_inter = jnp.exp(m_h - u)
        q = qk_ref[:, h * MLSTM_DK:(h + 1) * MLSTM_DK].astype(BF16)
        k = qk_ref[:, BRANCH + h * MLSTM_DK:BRANCH + (h + 1) * MLSTM_DK].astype(BF16)
        vt = jnp.concatenate([vt_ref[h * MLSTM_DK:(h + 1) * MLSTM_DK, :], ones], axis=0)
        sc = _dot_nt(k, q) * jnp.exp(a_m - (u - LOG_QK_SCALE))
        st = st_sc[h]
        nd = _dot(vt.astype(BF16), sc.astype(BF16)) + w_inter * _dot_nt(st.astype(BF16), q)
        den = jnp.maximum(jnp.abs(nd[MLSTM_DK:MLSTM_DK + 1, :]), jnp.exp(-m_t))
        outs.append(nd[:MLSTM_DK] * (1.0 / den))
        vw = (vt * wk[h:h + 1, :]).astype(BF16)
        st_sc[h] = decay[h:h + 1, :] * st + _dot(vw, k)
    hh = jnp.concatenate(outs, axis=0)

    @pl.when(d == 0)
    def _():
        o_ref[cidx] = hh

    @pl.when(d == 1)
    def _():
        tot = o_ref[cidx] + hh
        normed = []
        for h in range(MLSTM_HEADS):
            x = tot[h * MLSTM_DK:(h + 1) * MLSTM_DK]
            ms = jnp.mean(x * x, axis=0, keepdims=True)
            normed.append(x * lax.rsqrt(ms + RMS_EPS))
        o_ref[cidx] = jnp.concatenate(normed, axis=0) * ng_ref[...]


def _mlstm_call(mqk, p_t, b, s, dirp, ng_t):
    nc = s // CHUNK
    mqk3 = mqk.reshape(b, s, MQK_W)
    gate_blk = T_GATE // (2 * SUBLANES)
    return pl.pallas_call(
        _mlstm_kernel,
        grid=(b, 2, nc),
        in_specs=[
            pl.BlockSpec((None, CHUNK, MQK_W), lambda bi, d, c: (bi, _chunk_index(d, c, nc), 0)),
            pl.BlockSpec((None, BRANCH, CHUNK),
                         lambda bi, d, c: (bi * nc + _chunk_index(d, c, nc), T_MV // BRANCH, 0)),
            pl.BlockSpec((None, 2 * SUBLANES, CHUNK),
                         lambda bi, d, c: (bi * nc + _chunk_index(d, c, nc), gate_blk + d, 0)),
            pl.BlockSpec((None, 2 * SUBLANES, LANES), lambda bi, d, c: (d, 0, 0)),
            pl.BlockSpec((BRANCH, LANES), lambda bi, d, c: (0, 0)),
        ],
        out_specs=pl.BlockSpec((nc, BRANCH, CHUNK), lambda bi, d, c: (bi, 0, 0)),
        out_shape=jax.ShapeDtypeStruct((b * nc, BRANCH, CHUNK), F32),
        scratch_shapes=[
            pltpu.VMEM((MLSTM_HEADS, AUG, MLSTM_DK), F32),
            pltpu.VMEM((SUBLANES, LANES), F32),
        ],
        compiler_params=pltpu.CompilerParams(
            dimension_semantics=("arbitrary", "arbitrary", "arbitrary"), vmem_limit_bytes=VMEM_LIMIT),
        name="mlstm",
    )(mqk3, p_t, p_t, dirp, ng_t)


def _ssd_kernel(x_ref, xp_ref, xn_ref, g_ref, dp_ref, cw_ref, cb_ref, dsk_ref, o_ref, st_sc, pad_sc):
    d = pl.program_id(1)
    c = pl.program_id(2)
    nc = pl.num_programs(2)
    cidx = _chunk_index(d, c, nc)
    rows = pl.ds(pl.multiple_of(cidx * CHUNK, CHUNK), CHUNK)

    @pl.when(c == 0)
    def _():
        st_sc[...] = jnp.zeros_like(st_sc)

    pad_sc[0:HALO, :] = jnp.where(cidx > 0, xp_ref[...], 0.0)
    pad_sc[HALO:HALO + CHUNK, :] = x_ref[...]
    pad_sc[HALO + CHUNK:, :] = jnp.where(cidx < nc - 1, xn_ref[...], 0.0)
    acc = jnp.broadcast_to(cb_ref[...], (CHUNK, XBC_W))
    for t in range(SSM_CONV):
        off = HALO - SSM_CONV // 2 + t
        acc = acc + pad_sc[off:off + CHUNK, :] * cw_ref[t:t + 1, :]
    xc = _silu(acc)
    xs = xc[:, :BRANCH]
    bt = xc[:, BRANCH:BRANCH + LANES].T
    cm = xc[:, BRANCH + LANES:]

    causal, anti = _scan_masks(d)
    dt = _softplus(g_ref[...] + dp_ref[0:1, :])
    da = dt * (-jnp.exp(dp_ref[1:2, :]))
    dt_t = dt.T
    da_t = da.T
    acs_c = _cum_left(causal.astype(BF16), da)
    acs_r = _cum_right(da_t, anti.astype(BF16))
    tot = jnp.sum(da_t, axis=1, keepdims=True)

    lane = lax.broadcasted_iota(jnp.int32, (CHUNK, LANES), 1)
    row = lax.broadcasted_iota(jnp.int32, (CHUNK, LANES), 0)
    lo = lane < 64
    lo_row = lo[0:1, :]
    bt_b = bt.astype(BF16)
    cm_b = cm.astype(BF16)
    zero = jnp.zeros((CHUNK, LANES), F32)
    cb = [_dot(jnp.where(lo, cm, zero).astype(BF16), bt_b),
          _dot(jnp.where(lo, zero, cm).astype(BF16), bt_b)]

    ys = []
    for j in range(4):
        grp = j // 2
        m_parts, bw_parts, eac, edec = [], [], [], []
        for e in (2 * j, 2 * j + 1):
            ln = SSM_HEADS + e
            ac = acs_c[:, ln:ln + 1]
            ar = acs_r[ln:ln + 1, :]
            dtr = dt_t[ln:ln + 1, :]
            te = tot[ln:ln + 1, :]
            lm = jnp.exp(jnp.where(causal, ac - ar, NEG))
            m_parts.append((cb[grp] * lm * dtr).astype(BF16))
            bw_parts.append((bt * (jnp.exp(te - ar) * dtr)).astype(BF16))
            eac.append(jnp.exp(ac))
            edec.append(jnp.exp(te))
        lhs = jnp.concatenate([jnp.concatenate(m_parts, axis=1),
                               jnp.concatenate(bw_parts, axis=1)], axis=0)
        xp = xs[:, j * LANES:(j + 1) * LANES]
        x2 = jnp.concatenate([jnp.where(lo, xp, zero), jnp.where(lo, zero, xp)], axis=0).astype(BF16)
        res = _dot(lhs, x2)
        in_grp = (row >= 64 * grp) & (row < 64 * grp + 64)
        s_new = jnp.where(in_grp, res[CHUNK:], zero)
        prev = st_sc[j]
        y_off = _dot(cm_b, prev.astype(BF16)) * jnp.where(lo, eac[0], eac[1])
        ys.append(res[:CHUNK] + y_off)
        st_sc[j] = prev * jnp.where(lo_row, edec[0], edec[1]) + s_new
    y = jnp.concatenate(ys, axis=1)

    @pl.when(d == 0)
    def _():
        o_ref[rows, :] = y + dsk_ref[...] * xs

    @pl.when(d == 1)
    def _():
        o_ref[rows, :] = o_ref[rows, :] + y


def _ssd_call(xbc, small, b, s, dirp, conv_w8, conv_b, dskip):
    nc = s // CHUNK
    per = CHUNK // HALO
    last = s // HALO - 1
    xbc3 = xbc.reshape(b, s, XBC_W)
    small3 = small.reshape(b, s, SMALL_W)

    def cur(bi, d, c):
        return (bi, _chunk_index(d, c, nc), 0)

    def prev(bi, d, c):
        return (bi, jnp.maximum(_chunk_index(d, c, nc) * per - 1, 0), 0)

    def nxt(bi, d, c):
        return (bi, jnp.minimum(_chunk_index(d, c, nc) * per + per, last), 0)

    return pl.pallas_call(
        _ssd_kernel,
        grid=(b, 2, nc),
        in_specs=[
            pl.BlockSpec((None, CHUNK, XBC_W), cur),
            pl.BlockSpec((None, HALO, XBC_W), prev),
            pl.BlockSpec((None, HALO, XBC_W), nxt),
            pl.BlockSpec((None, CHUNK, LANES), lambda bi, d, c: (bi, _chunk_index(d, c, nc), 1 + d)),
            pl.BlockSpec((None, SUBLANES, LANES), lambda bi, d, c: (d, 0, 0)),
            pl.BlockSpec((SUBLANES, XBC_W), lambda bi, d, c: (0, 0)),
            pl.BlockSpec((1, XBC_W), lambda bi, d, c: (0, 0)),
            pl.BlockSpec((1, BRANCH), lambda bi, d, c: (0, 0)),
        ],
        out_specs=pl.BlockSpec((None, s, BRANCH), lambda bi, d, c: (bi, 0, 0)),
        out_shape=jax.ShapeDtypeStruct((b, s, BRANCH), F32),
        scratch_shapes=[
            pltpu.VMEM((4, CHUNK, LANES), F32),
            pltpu.VMEM((CHUNK + 2 * HALO, XBC_W), F32),
        ],
        compiler_params=pltpu.CompilerParams(
            dimension_semantics=("arbitrary", "arbitrary", "arbitrary"), vmem_limit_bytes=VMEM_LIMIT),
        name="ssd",
    )(xbc3, xbc3, xbc3, small3, dirp, conv_w8, conv_b, dskip)


def _final_kernel(x_ref, yat_ref, hmt_ref, ys_ref, ng_ref, sg_ref, w2_ref, wa_ref, wb_ref, wc_ref,
                  wo_ref, o_ref):
    x = x_ref[...]
    h = _rms(x, ng_ref[...]).astype(BF16)

    def proj(k):
        return _dot(h, w2_ref[:, k * BRANCH:(k + 1) * BRANCH])

    def gate(k):
        lo = 4 * BRANCH + k * D_MODEL
        return _sigmoid(_dot(h, w2_ref[:, lo:lo + D_MODEL]))

    def token_major(ref):
        return jnp.concatenate([ref[j].T for j in range(FINAL_TM // CHUNK)], axis=0)

    ya = token_major(yat_ref) * _silu(proj(0))
    merged = gate(0) * _dot(ya.astype(BF16), wa_ref[...])
    yb = token_major(hmt_ref) * _sigmoid(proj(1)) * _silu(proj(2))
    merged = merged + gate(1) * _dot(yb.astype(BF16), wb_ref[...])
    yc = _rms(ys_ref[...] * _silu(proj(3)), sg_ref[...])
    merged = merged + gate(2) * _dot(yc.astype(BF16), wc_ref[...])
    o_ref[...] = x + _dot(merged.astype(BF16), wo_ref[...])


def _final_call(x2, ya_t, hm_t, ys, ng, sg, w2, wa, wb, wc, wo):
    t = x2.shape[0]
    tm = FINAL_TM
    row = lambda width: pl.BlockSpec((tm, width), lambda i: (i, 0))
    chunked = pl.BlockSpec((tm // CHUNK, BRANCH, CHUNK), lambda i: (i, 0, 0))
    full = lambda shape: pl.BlockSpec(shape, lambda i: (0, 0))
    return pl.pallas_call(
        _final_kernel,
        grid=(t // tm,),
        in_specs=[
            row(D_MODEL), chunked, chunked, row(BRANCH),
            full((1, D_MODEL)), full((1, BRANCH)),
            full((D_MODEL, W2_WIDTH)), full((BRANCH, D_MODEL)), full((BRANCH, D_MODEL)),
            full((BRANCH, D_MODEL)), full((D_MODEL, D_MODEL)),
        ],
        out_specs=row(D_MODEL),
        out_shape=jax.ShapeDtypeStruct((t, D_MODEL), F32),
        compiler_params=pltpu.CompilerParams(
            dimension_semantics=("arbitrary",), vmem_limit_bytes=VMEM_LIMIT),
        name="final",
    )(x2, ya_t, hm_t, ys, ng, sg, w2, wa, wb, wc, wo)


_IN_SPLITS = (512, 128, 128, 512, 512, 512, 512, 512, 8, 8, 512, 768, 16, 512, 3072)
_NAMES = ("aq", "ak", "av", "az", "mq", "mk", "mv", "mo", "mi", "mf", "mz", "sxbc", "sdt", "sz", "gates")


def _split_w_in(w):
    bounds = np.cumsum((0,) + _IN_SPLITS)
    return {n: w[:, bounds[i]:bounds[i + 1]] for i, n in enumerate(_NAMES)}


def _layer_params(w_in, q_norm_g, k_norm_g, mlstm_i_b, mlstm_f_b, mlstm_norm_g, a_log, dt_bias, conv_w,
                  conv_b, d_skip):
    p = _split_w_in(w_in)
    zc = lambda n: jnp.zeros((D_MODEL, n), F32)
    ssd_gates = []
    for d in range(2):
        ssd_gates += [zc(SSM_HEADS), p["sdt"][:, 8 * d:8 * d + 8], zc(LANES - 2 * SSM_HEADS)]
    w_row = jnp.concatenate([p["mq"], p["mk"], p["sxbc"], p["ak"]] + ssd_gates, axis=1).astype(BF16)
    m_gates = []
    for d in range(2):
        m_gates += [p["mi"][:, 4 * d:4 * d + 4], zc(4), p["mf"][:, 4 * d:4 * d + 4], zc(4)]
    w_t = jnp.concatenate([p["mv"], p["aq"], p["av"]] + m_gates, axis=1).T.astype(BF16)
    w2 = jnp.concatenate([p["az"], p["mo"], p["mz"], p["sz"], p["gates"]], axis=1).astype(BF16)
    sdirp = jnp.zeros((2, SUBLANES, LANES), F32)
    sdirp = sdirp.at[:, 0, 8:16].set(dt_bias).at[:, 1, 8:16].set(a_log)
    mdirp = jnp.zeros((2, 2 * SUBLANES, LANES), F32)
    mdirp = mdirp.at[:, 0:4, :].set(mlstm_i_b[:, :, None]).at[:, 8:12, :].set(mlstm_f_b[:, :, None])
    return dict(
        w_row=w_row, w_t=w_t, w2=w2, sdirp=sdirp, mdirp=mdirp,
        qg_t=jnp.broadcast_to(q_norm_g[:, None], (ATT_HEAD_DIM, LANES)),
        kg=jnp.tile(k_norm_g, 2)[None, :],
        ng_t=jnp.broadcast_to(mlstm_norm_g[:, None], (BRANCH, LANES)),
        conv_w8=jnp.zeros((SUBLANES, XBC_W), F32).at[:SSM_CONV].set(conv_w), conv_b=conv_b[None, :],
        dskip=jnp.repeat(d_skip, 64)[None, :])


def _rope_tables(s_len):
    inv_freq = ROPE_THETA ** (-jnp.arange(ROT_HALF, dtype=F32) * 2.0 / (2 * ROT_HALF))
    ang = jnp.arange(s_len).astype(F32)[:, None] * inv_freq
    cos, sin = jnp.cos(ang), jnp.sin(ang)
    rest = ATT_HEAD_DIM - 2 * ROT_HALF
    one = jnp.ones((s_len, rest), F32)
    zer = jnp.zeros((s_len, rest), F32)
    z8 = jnp.zeros((s_len, ROT_HALF), F32)
    tc = jnp.concatenate([cos, cos, one], axis=1)
    t1 = jnp.concatenate([-sin, z8, zer], axis=1)
    t2 = jnp.concatenate([z8, sin, zer], axis=1)
    rows = jnp.concatenate([jnp.tile(t, (1, 2)) for t in (tc, t1, t2)], axis=1)
    nb = s_len // CHUNK
    t_tab = jnp.concatenate([cos.T.reshape(ROT_HALF, nb, CHUNK), sin.T.reshape(ROT_HALF, nb, CHUNK)], axis=0)
    return rows, t_tab.transpose(1, 0, 2)


def _block_diag_mean(width):
    blk = np.kron(np.eye(width // ATT_HEAD_DIM), np.full((ATT_HEAD_DIM, ATT_HEAD_DIM), 1.0 / ATT_HEAD_DIM))
    return jnp.asarray(blk, BF16)


def _layer(x, lp, ropes, bdk, norm_g, attn_sink, ssm_norm_g, wa, wb, wc, wo):
    b, s, _ = x.shape
    assert s % CHUNK == 0 and s >= 2 * CHUNK and (b * s) % PROJ_TM == 0, (b, s)
    x2 = x.reshape(b * s, D_MODEL)
    mqk, xbc, small, p_t = _proj_call(x2, norm_g[None, :], lp["w_row"], lp["w_t"])
    ya_t = _att_call(small, p_t, b, s, ropes[0], ropes[1], lp["qg_t"], lp["kg"], bdk, attn_sink)
    hm_t = _mlstm_call(mqk, p_t, b, s, lp["mdirp"], lp["ng_t"])
    ys = _ssd_call(xbc, small, b, s, lp["sdirp"], lp["conv_w8"], lp["conv_b"], lp["dskip"])
    out = _final_call(x2, ya_t, hm_t, ys.reshape(b * s, BRANCH), norm_g[None, :], ssm_norm_g[None, :],
                      lp["w2"], wa, wb, wc, wo)
    return out.reshape(b, s, D_MODEL)


def kernel(x_prompt, x_sample, norm_g, w_in, q_norm_g, k_norm_g, attn_sink, w_att_out, mlstm_i_b, mlstm_f_b,
           mlstm_norm_g, w_mlstm_out, conv_w, conv_b, a_log, dt_bias, d_skip, ssm_norm_g, w_ssm_out, w_out):
    depth = w_in.shape[0]
    bdk = _block_diag_mean(LANES)
    ropes = {x.shape[1]: _rope_tables(x.shape[1]) for x in (x_prompt, x_sample)}
    ys = [x_prompt, x_sample]
    for l in range(depth):
        lp = _layer_params(w_in[l], q_norm_g[l], k_norm_g[l], mlstm_i_b[l], mlstm_f_b[l], mlstm_norm_g[l],
                           a_log[l], dt_bias[l], conv_w[l], conv_b[l], d_skip[l])
        wa, wb, wc, wo = (w.astype(BF16) for w in (w_att_out[l], w_mlstm_out[l], w_ssm_out[l], w_out[l]))
        ys = [_layer(y, lp, ropes[y.shape[1]], bdk, norm_g[l], attn_sink[l], ssm_norm_g[l], wa, wb, wc, wo)
              for y in ys]
    return tuple(ys)
```

```python
import functools

import jax
import jax.numpy as jnp
import numpy as np
from jax import lax
from jax.experimental import pallas as pl
from jax.experimental.pallas import tpu as pltpu

F32 = jnp.float32
BF16 = jnp.bfloat16

D_MODEL = 1024
BRANCH = 512
RMS_EPS = 1e-6
CHUNK = 128
LANES = 128
SUBLANES = 8
BF16_ROWS = 16
ATT_HEADS = 8
ATT_HEAD_DIM = 64
ROT_HALF = 8
ROPE_THETA = 500000.0
MLSTM_HEADS = 4
MLSTM_DK = 128
MLSTM_BLOCK = 4
SSM_HEADS = 8
SSM_CONV = 5
HALO = BF16_ROWS
NEG = -1e30

MQK_W = 1024
XBC_W = 768
SMALL_W = 384
ROW_W = MQK_W + XBC_W + SMALL_W
T_MV, T_AQ, T_AV = 0, 512, 1024
T_ROWS = 1152
G_ROWS = 32
W2_WIDTH = 5120
PROJ_TM = 512
FINAL_TM = 256

VMEM_LIMIT = 56 * 1024 * 1024


def _dot(a, b):
    return jnp.dot(a, b, preferred_element_type=F32)


def _dot_nt(a, b):
    return lax.dot_general(a, b, (((1,), (1,)), ((), ())), preferred_element_type=F32)


def _split3(x):
    h1 = x.astype(BF16)
    r1 = x - h1.astype(F32)
    h2 = r1.astype(BF16)
    r2 = r1 - h2.astype(F32)
    return h1, h2, r2.astype(BF16)


def _cum_left(mask_bf, x):
    h1, h2, h3 = _split3(x)
    return _dot(mask_bf, h1) + _dot(mask_bf, h2) + _dot(mask_bf, h3)


def _cum_right(x, mask_bf):
    h1, h2, h3 = _split3(x)
    return _dot(h1, mask_bf) + _dot(h2, mask_bf) + _dot(h3, mask_bf)


def _sigmoid(x):
    return 1.0 / (1.0 + jnp.exp(-x))


def _silu(x):
    return x * _sigmoid(x)


def _softplus(x):
    return jnp.maximum(x, 0.0) + jnp.log1p(jnp.exp(-jnp.abs(x)))


def _log_sigmoid(x):
    return -_softplus(-x)


def _rms(x, g_row):
    ms = jnp.mean(x * x, axis=-1, keepdims=True)
    return x * lax.rsqrt(ms + RMS_EPS) * g_row


def _params(n_axes):
    return pltpu.CompilerParams(dimension_semantics=("arbitrary",) * n_axes, vmem_limit_bytes=VMEM_LIMIT)


def _proj_kernel(x_ref, ng_ref, wr_ref, wt_ref, wg_ref, mqk_ref, xbc_ref, small_ref, t_ref, g_ref):
    h = _rms(x_ref[...], ng_ref[...]).astype(BF16)
    for j0 in range(0, MQK_W, 512):
        mqk_ref[:, j0:j0 + 512] = _dot(h, wr_ref[:, j0:j0 + 512]).astype(BF16)
    xbc_ref[...] = _dot(h, wr_ref[:, MQK_W:MQK_W + XBC_W]).astype(BF16)
    small_ref[...] = _dot(h, wr_ref[:, MQK_W + XBC_W:])
    n_chunks = PROJ_TM // CHUNK
    for r0 in range(0, T_ROWS, 384):
        res = _dot_nt(wt_ref[r0:r0 + 384, :], h).astype(BF16)
        for j in range(n_chunks):
            t_ref[j, r0:r0 + 384, :] = res[:, j * CHUNK:(j + 1) * CHUNK]
    gates = _dot_nt(wg_ref[...], h)
    for j in range(n_chunks):
        g_ref[j] = gates[:, j * CHUNK:(j + 1) * CHUNK]


def _proj_call(x2, ng, w_row, w_t, w_g):
    t = x2.shape[0]
    tm = PROJ_TM
    full = lambda shape: pl.BlockSpec(shape, lambda i: (0, 0))
    rows = lambda width: pl.BlockSpec((tm, width), lambda i: (i, 0))
    chunks = lambda r: pl.BlockSpec((tm // CHUNK, r, CHUNK), lambda i: (i, 0, 0))
    return pl.pallas_call(
        _proj_kernel,
        grid=(t // tm,),
        in_specs=[rows(D_MODEL), full((1, D_MODEL)), full((D_MODEL, ROW_W)), full((T_ROWS, D_MODEL)),
                  full((G_ROWS, D_MODEL))],
        out_specs=[rows(MQK_W), rows(XBC_W), rows(SMALL_W), chunks(T_ROWS), chunks(G_ROWS)],
        out_shape=[
            jax.ShapeDtypeStruct((t, MQK_W), BF16),
            jax.ShapeDtypeStruct((t, XBC_W), BF16),
            jax.ShapeDtypeStruct((t, SMALL_W), F32),
            jax.ShapeDtypeStruct((t // CHUNK, T_ROWS, CHUNK), BF16),
            jax.ShapeDtypeStruct((t // CHUNK, G_ROWS, CHUNK), F32),
        ],
        compiler_params=_params(1),
        name="proj",
    )(x2, ng, w_row, w_t, w_g)


def _rope_rows(x, tab):
    w = x.shape[1]
    return (x * tab[:, :w]
            + pltpu.roll(x, w - ROT_HALF, 1) * tab[:, w:2 * w]
            + pltpu.roll(x, ROT_HALF, 1) * tab[:, 2 * w:])


def _att_kernel(kp_ref, kc_ref, kn_ref, qt_ref, vp_ref, vc_ref, vn_ref, rp_ref, rc_ref, rn_ref,
                rt_ref, qg_ref, kg_ref, bdk_ref, sink_ref, o_ref):
    i = pl.program_id(1)
    nb = pl.num_programs(1)
    win = 3 * CHUNK

    k = jnp.concatenate([kp_ref[...], kc_ref[...], kn_ref[...]], axis=0)
    tab_k = jnp.concatenate([rp_ref[...], rc_ref[...], rn_ref[...]], axis=0)
    kms = _dot((k * k).astype(BF16), bdk_ref[...])
    kr = _rope_rows(k * lax.rsqrt(kms + RMS_EPS) * kg_ref[...], tab_k)
    lane = lax.broadcasted_iota(jnp.int32, (win, LANES), 1)
    lo = lane < ATT_HEAD_DIM
    k_sw = pltpu.roll(kr, ATT_HEAD_DIM, 1)
    zero = jnp.zeros_like(kr)
    k_var = [[jnp.where(lo, kr, zero).astype(BF16), jnp.where(lo, zero, k_sw).astype(BF16)],
             [jnp.where(lo, k_sw, zero).astype(BF16), jnp.where(lo, zero, kr).astype(BF16)]]

    v_t = jnp.concatenate([vp_ref[...], vc_ref[...], vn_ref[...]], axis=1)

    cos = rt_ref[0:ROT_HALF, :]
    sin = rt_ref[ROT_HALF:, :]
    q_heads = []
    for h in range(ATT_HEADS):
        x = qt_ref[h * ATT_HEAD_DIM:(h + 1) * ATT_HEAD_DIM, :].astype(F32)
        ms = jnp.mean(x * x, axis=0, keepdims=True)
        xn = x * lax.rsqrt(ms + RMS_EPS) * qg_ref[...]
        x1 = xn[0:ROT_HALF]
        x2 = xn[ROT_HALF:2 * ROT_HALF]
        xr = jnp.concatenate([x1 * cos - x2 * sin, x2 * cos + x1 * sin, xn[2 * ROT_HALF:]], axis=0)
        q_heads.append((xr * (ATT_HEAD_DIM ** -0.5)).astype(BF16))

    s_idx = lax.broadcasted_iota(jnp.int32, (win, 2 * CHUNK), 0)
    t_idx = lax.broadcasted_iota(jnp.int32, (win, 2 * CHUNK), 1) % CHUNK
    valid = ((s_idx >= t_idx) & (s_idx <= t_idx + 2 * CHUNK)
             & ((s_idx >= CHUNK) | (i > 0)) & ((s_idx < 2 * CHUNK) | (i < nb - 1)))
    bias = jnp.where(valid, 0.0, NEG)
    first_head = lax.broadcasted_iota(jnp.int32, (1, 2 * CHUNK), 1) < CHUNK

    outs = [None] * ATT_HEADS
    s2 = []
    for g in range(2):
        lhs = jnp.concatenate(k_var[g], axis=0)
        rhs = jnp.concatenate(
            [jnp.concatenate([q_heads[4 * g + 2 * j], q_heads[4 * g + 2 * j + 1]], axis=0) for j in range(2)],
            axis=1)
        s2.append(_dot(lhs, rhs))
    ps, dens = [], []
    for g in range(2):
        for half in range(2):
            s = s2[g][half * win:(half + 1) * win] + bias
            sink = jnp.where(first_head, sink_ref[4 * g + half], sink_ref[4 * g + 2 + half])
            mx = jnp.maximum(jnp.max(s, axis=0, keepdims=True), sink)
            p = jnp.exp(s - mx)
            dens.append(jnp.sum(p, axis=0, keepdims=True) + jnp.exp(sink - mx))
            ps.append(p.astype(BF16))
    for g in range(2):
        vg = v_t[g * ATT_HEAD_DIM:(g + 1) * ATT_HEAD_DIM, :]
        for half in range(2):
            o = _dot(vg, ps[2 * g + half]) * (1.0 / dens[2 * g + half])
            outs[4 * g + half] = o[:, :CHUNK]
            outs[4 * g + 2 + half] = o[:, CHUNK:]
    o_ref[...] = jnp.concatenate(outs, axis=0)


def _att_call(small, p_t, b, s, rope_rows, rope_t, qg_t, kg, bdk, sink):
    nb = s // CHUNK
    small3 = small.reshape(b, s, SMALL_W)
    prev = lambda i: jnp.maximum(i - 1, 0)
    nxt = lambda i: jnp.minimum(i + 1, nb - 1)
    kspec = lambda f: pl.BlockSpec((None, CHUNK, LANES), lambda bi, i: (bi, f(i), 0))
    vspec = lambda f: pl.BlockSpec((None, LANES, CHUNK), lambda bi, i: (bi * nb + f(i), T_AV // LANES, 0))
    rspec = lambda f: pl.BlockSpec((CHUNK, 3 * LANES), lambda bi, i: (f(i), 0))
    same = lambda i: i
    return pl.pallas_call(
        _att_kernel,
        grid=(b, nb),
        in_specs=[
            kspec(prev), kspec(same), kspec(nxt),
            pl.BlockSpec((None, BRANCH, CHUNK), lambda bi, i: (bi * nb + i, T_AQ // BRANCH, 0)),
            vspec(prev), vspec(same), vspec(nxt),
            rspec(prev), rspec(same), rspec(nxt),
            pl.BlockSpec((None, 2 * ROT_HALF, CHUNK), lambda bi, i: (i, 0, 0)),
            pl.BlockSpec((ATT_HEAD_DIM, LANES), lambda bi, i: (0, 0)),
            pl.BlockSpec((1, LANES), lambda bi, i: (0, 0)),
            pl.BlockSpec((LANES, LANES), lambda bi, i: (0, 0)),
            pl.BlockSpec(memory_space=pltpu.SMEM),
        ],
        out_specs=pl.BlockSpec((None, BRANCH, CHUNK), lambda bi, i: (bi * nb + i, 0, 0)),
        out_shape=jax.ShapeDtypeStruct((b * nb, BRANCH, CHUNK), F32),
        compiler_params=_params(2),
        name="att",
    )(small3, small3, small3, p_t, p_t, p_t, p_t, rope_rows, rope_rows, rope_rows, rope_t,
      qg_t, kg, bdk, sink)


def _scan_masks(reverse):
    r = lax.broadcasted_iota(jnp.int32, (CHUNK, CHUNK), 0)
    c = lax.broadcasted_iota(jnp.int32, (CHUNK, CHUNK), 1)
    causal = (r <= c) if reverse else (r >= c)
    anti = (r >= c) if reverse else (r <= c)
    return causal, anti


def _scan_block(reverse, n):
    return (lambda c: n - 1 - c) if reverse else (lambda c: c)


AUG = MLSTM_DK + BF16_ROWS
LOG_QK_SCALE = float(np.log(MLSTM_DK ** -0.5))


def _mlstm_kernel(reverse, qk_ref, vt_ref, g_ref, dp_ref, *rest):
    if reverse:
        prev_ref, ng_ref, o_ref, st_sc, m_sc = rest
    else:
        o_ref, st_sc, m_sc = rest
    nblk = MLSTM_BLOCK
    rows = nblk * SUBLANES

    @pl.when(pl.program_id(1) == 0)
    def _():
        st_sc[...] = jnp.zeros_like(st_sc)
        m_sc[...] = jnp.zeros_like(m_sc)

    _, anti = _scan_masks(reverse)
    g = g_ref[...]
    ig = (g[:, 0:SUBLANES, :] + dp_ref[0:SUBLANES, :]).reshape(rows, CHUNK)
    lf = _log_sigmoid(g[:, SUBLANES:, :] + dp_ref[SUBLANES:, :]).reshape(rows, CHUNK)
    b = _cum_right(lf, anti.astype(BF16))
    a = ig - b
    a_max = jnp.max(a, axis=1, keepdims=True)
    b_tot = jnp.sum(lf, axis=1, keepdims=True)
    a_col = jnp.concatenate([a, jnp.zeros((CHUNK - rows, CHUNK), F32)], axis=0).T
    ones = jnp.ones((AUG - MLSTM_DK, CHUNK), BF16)
    scale = MLSTM_DK ** -0.5

    m_st = m_sc[...][:, 0:1]
    st = [st_sc[h] for h in range(MLSTM_HEADS)]
    order = range(nblk - 1, -1, -1) if reverse else range(nblk)
    for j in order:
        sl = slice(j * SUBLANES, (j + 1) * SUBLANES)
        tok = slice(j * CHUNK, (j + 1) * CHUNK)
        mm = jnp.maximum(m_st, a_max[sl])
        wk = jnp.exp(a[sl] - mm) * scale
        decay = jnp.exp(m_st - mm)
        ks, vts, sts = [], [], []
        for h in range(MLSTM_HEADS):
            q = qk_ref[tok, h * MLSTM_DK:(h + 1) * MLSTM_DK]
            k = qk_ref[tok, BRANCH + h * MLSTM_DK:BRANCH + (h + 1) * MLSTM_DK]
            ks.append(k)
            vts.append(jnp.concatenate([vt_ref[j, h * MLSTM_DK:(h + 1) * MLSTM_DK, :], ones], axis=0))
            sts.append(_dot_nt(jnp.concatenate([k, st[h].astype(BF16)], axis=0), q))
        scs, w_inters, m_ts = [], [], []
        for h in range(MLSTM_HEADS):
            col = j * SUBLANES + h
            a_m = jnp.where(anti, jnp.broadcast_to(a_col[:, col:col + 1], (CHUNK, CHUNK)), NEG)
            m_h = m_st[h:h + 1, :]
            u = jnp.maximum(m_h, jnp.max(a_m, axis=0, keepdims=True))
            m_ts.append(b[col:col + 1, :] + u)
            w_inters.append(jnp.exp(m_h - u))
            scs.append((sts[h][:CHUNK] * jnp.exp(a_m - (u - LOG_QK_SCALE))).astype(BF16))
        outs = []
        for h in range(MLSTM_HEADS):
            nd = _dot(vts[h], scs[h]) + w_inters[h] * sts[h][CHUNK:]
            den = jnp.maximum(jnp.abs(nd[MLSTM_DK:MLSTM_DK + 1, :]), jnp.exp(-m_ts[h]))
            outs.append(nd[:MLSTM_DK] * (1.0 / den))
        for h in range(MLSTM_HEADS):
            vw = (vts[h].astype(F32) * wk[h:h + 1, :]).astype(BF16)
            st[h] = decay[h:h + 1, :] * st[h] + _dot(vw, ks[h])
        m_st = b_tot[sl] + mm
        if reverse:
            normed = []
            for h in range(MLSTM_HEADS):
                x = prev_ref[j, h * MLSTM_DK:(h + 1) * MLSTM_DK, :] + outs[h]
                ms = jnp.mean(x * x, axis=0, keepdims=True)
                normed.append(x * lax.rsqrt(ms + RMS_EPS))
            o_ref[j] = jnp.concatenate(normed, axis=0) * ng_ref[...]
        else:
            o_ref[j] = jnp.concatenate(outs, axis=0)
    for h in range(MLSTM_HEADS):
        st_sc[h] = st[h]
    m_sc[...] = jnp.broadcast_to(m_st, (SUBLANES, LANES))


def _mlstm_call(reverse, mqk, p_t, g_t, b, s, dirp, prev=None, ng_t=None):
    nb = s // (CHUNK * MLSTM_BLOCK)
    blk = _scan_block(reverse, nb)
    mqk3 = mqk.reshape(b, s, MQK_W)
    d = int(reverse)
    out_spec = pl.BlockSpec((MLSTM_BLOCK, BRANCH, CHUNK), lambda bi, c: (bi * nb + blk(c), 0, 0))
    in_specs = [
        pl.BlockSpec((None, CHUNK * MLSTM_BLOCK, MQK_W), lambda bi, c: (bi, blk(c), 0)),
        pl.BlockSpec((MLSTM_BLOCK, BRANCH, CHUNK), lambda bi, c: (bi * nb + blk(c), T_MV // BRANCH, 0)),
        pl.BlockSpec((MLSTM_BLOCK, 2 * SUBLANES, CHUNK), lambda bi, c: (bi * nb + blk(c), d, 0)),
        pl.BlockSpec((None, 2 * SUBLANES, LANES), lambda bi, c: (d, 0, 0)),
    ]
    args = [mqk3, p_t, g_t, dirp]
    if reverse:
        in_specs += [out_spec, pl.BlockSpec((BRANCH, LANES), lambda bi, c: (0, 0))]
        args += [prev, ng_t]
    return pl.pallas_call(
        functools.partial(_mlstm_kernel, reverse),
        grid=(b, nb),
        in_specs=in_specs,
        out_specs=out_spec,
        out_shape=jax.ShapeDtypeStruct((b * s // CHUNK, BRANCH, CHUNK), F32),
        scratch_shapes=[
            pltpu.VMEM((MLSTM_HEADS, AUG, MLSTM_DK), F32),
            pltpu.VMEM((SUBLANES, LANES), F32),
        ],
        compiler_params=_params(2),
        name="mlstm_bwd" if reverse else "mlstm_fwd",
    )(*args)


def _ssd_scan(reverse, xs_b, bt, cm_b, g_ref, dp_ref, st_sc):
    causal, anti = _scan_masks(reverse)
    dt = _softplus(g_ref[...] + dp_ref[0:1, :])
    da = dt * (-jnp.exp(dp_ref[1:2, :]))
    dt_t = dt.T
    da_t = da.T
    acs_c = _cum_left(causal.astype(BF16), da)
    acs_r = _cum_right(da_t, anti.astype(BF16))
    tot = jnp.sum(da_t, axis=1, keepdims=True)

    lane = lax.broadcasted_iota(jnp.int32, (CHUNK, LANES), 1)
    row = lax.broadcasted_iota(jnp.int32, (CHUNK, LANES), 0)
    lo = lane < 64
    lo_row = lo[0:1, :]
    bt_b = bt.astype(BF16)
    zero_b = jnp.zeros((CHUNK, LANES), BF16)
    zero = jnp.zeros((CHUNK, LANES), F32)
    cb = [_dot(jnp.where(lo, cm_b, zero_b), bt_b), _dot(jnp.where(lo, zero_b, cm_b), bt_b)]

    ys = []
    for j in range(4):
        grp = j // 2
        m_parts, bw_parts, eac, edec = [], [], [], []
        for e in (2 * j, 2 * j + 1):
            ln = SSM_HEADS + e
            ac = acs_c[:, ln:ln + 1]
            ar = acs_r[ln:ln + 1, :]
            dtr = dt_t[ln:ln + 1, :]
            te = tot[ln:ln + 1, :]
            lm = jnp.exp(jnp.where(causal, ac - ar, NEG))
            m_parts.append((cb[grp] * lm * dtr).astype(BF16))
            bw_parts.append((bt * (jnp.exp(te - ar) * dtr)).astype(BF16))
            eac.append(jnp.exp(ac))
            edec.append(jnp.exp(te))
        lhs = jnp.concatenate([jnp.concatenate(m_parts, axis=1),
                               jnp.concatenate(bw_parts, axis=1)], axis=0)
        xp = xs_b[:, j * LANES:(j + 1) * LANES]
        x2 = jnp.concatenate([jnp.where(lo, xp, zero_b), jnp.where(lo, zero_b, xp)], axis=0)
        res = _dot(lhs, x2)
        in_grp = (row >= 64 * grp) & (row < 64 * grp + 64)
        s_new = jnp.where(in_grp, res[CHUNK:], zero)
        prev = st_sc[j]
        y_off = _dot(cm_b, prev.astype(BF16)) * jnp.where(lo, eac[0], eac[1])
        ys.append(res[:CHUNK] + y_off)
        st_sc[j] = prev * jnp.where(lo_row, edec[0], edec[1]) + s_new
    return jnp.concatenate(ys, axis=1)


def _ssd_fwd_kernel(x_ref, xp_ref, xn_ref, g_ref, dp_ref, cw_ref, cb_ref, dsk_ref, y_ref, xc_ref,
                    st_sc, pad_sc):
    c = pl.program_id(1)
    nc = pl.num_programs(1)

    @pl.when(c == 0)
    def _():
        st_sc[...] = jnp.zeros_like(st_sc)

    pad_sc[0:HALO, :] = jnp.where(c > 0, xp_ref[...].astype(F32), 0.0)
    pad_sc[HALO:HALO + CHUNK, :] = x_ref[...].astype(F32)
    pad_sc[HALO + CHUNK:, :] = jnp.where(c < nc - 1, xn_ref[...].astype(F32), 0.0)
    acc = jnp.broadcast_to(cb_ref[...], (CHUNK, XBC_W))
    for t in range(SSM_CONV):
        off = HALO - SSM_CONV // 2 + t
        acc = acc + pad_sc[off:off + CHUNK, :] * cw_ref[t:t + 1, :]
    xc = _silu(acc)
    xc_b = xc.astype(BF16)
    xc_ref[...] = xc_b
    y = _ssd_scan(False, xc_b[:, :BRANCH], xc[:, BRANCH:BRANCH + LANES].T, xc_b[:, BRANCH + LANES:],
                  g_ref, dp_ref, st_sc)
    y_ref[...] = y + dsk_ref[...] * xc[:, :BRANCH]


def _ssd_bwd_kernel(xc_ref, g_ref, dp_ref, prev_ref, y_ref, st_sc):
    @pl.when(pl.program_id(1) == 0)
    def _():
        st_sc[...] = jnp.zeros_like(st_sc)

    bt = xc_ref[:, BRANCH:BRANCH + LANES].astype(F32).T
    y = _ssd_scan(True, xc_ref[:, :BRANCH], bt, xc_ref[:, BRANCH + LANES:], g_ref, dp_ref, st_sc)
    y_ref[...] = prev_ref[...] + y


def _ssd_calls(xbc, small, b, s, dirp, conv_w8, conv_b, dskip):
    nc = s // CHUNK
    per = CHUNK // HALO
    last = s // HALO - 1
    xbc3 = xbc.reshape(b, s, XBC_W)
    small3 = small.reshape(b, s, SMALL_W)
    state = pltpu.VMEM((4, CHUNK, LANES), F32)
    y_shape = jax.ShapeDtypeStruct((b, s, BRANCH), F32)
    const = lambda shape: pl.BlockSpec(shape, lambda bi, c: (0,) * len(shape))

    def chunk_spec(width, blk, col=0):
        return pl.BlockSpec((None, CHUNK, width), lambda bi, c: (bi, blk(c), col))

    fwd = _scan_block(False, nc)
    y_fwd, xc = pl.pallas_call(
        _ssd_fwd_kernel,
        grid=(b, nc),
        in_specs=[
            chunk_spec(XBC_W, fwd),
            pl.BlockSpec((None, HALO, XBC_W), lambda bi, c: (bi, jnp.maximum(c * per - 1, 0), 0)),
            pl.BlockSpec((None, HALO, XBC_W), lambda bi, c: (bi, jnp.minimum(c * per + per, last), 0)),
            chunk_spec(LANES, fwd, 1),
            pl.BlockSpec((None, SUBLANES, LANES), lambda bi, c: (0, 0, 0)),
            const((SUBLANES, XBC_W)), const((1, XBC_W)), const((1, BRANCH)),
        ],
        out_specs=[chunk_spec(BRANCH, fwd), chunk_spec(XBC_W, fwd)],
        out_shape=[y_shape, jax.ShapeDtypeStruct((b, s, XBC_W), BF16)],
        scratch_shapes=[state, pltpu.VMEM((CHUNK + 2 * HALO, XBC_W), F32)],
        compiler_params=_params(2),
        name="ssd_fwd",
    )(xbc3, xbc3, xbc3, small3, dirp, conv_w8, conv_b, dskip)

    bwd = _scan_block(True, nc)
    return pl.pallas_call(
        _ssd_bwd_kernel,
        grid=(b, nc),
        in_specs=[
            chunk_spec(XBC_W, bwd),
            chunk_spec(LANES, bwd, 2),
            pl.BlockSpec((None, SUBLANES, LANES), lambda bi, c: (1, 0, 0)),
            chunk_spec(BRANCH, bwd),
        ],
        out_specs=chunk_spec(BRANCH, bwd),
        out_shape=y_shape,
        scratch_shapes=[state],
        compiler_params=_params(2),
        name="ssd_bwd",
    )(xc, small3, dirp, y_fwd)


def _final_kernel(x_ref, yat_ref, hmt_ref, ys_ref, ng_ref, sg_ref, w2_ref, wa_ref, wb_ref, wc_ref,
                  wo_ref, o_ref):
    x = x_ref[...]
    h = _rms(x, ng_ref[...]).astype(BF16)

    def proj(k):
        return _dot(h, w2_ref[:, k * BRANCH:(k + 1) * BRANCH])

    def gate(k):
        lo = 4 * BRANCH + k * D_MODEL
        return _sigmoid(_dot(h, w2_ref[:, lo:lo + D_MODEL]))

    def token_major(ref):
        return jnp.concatenate([ref[j].T for j in range(FINAL_TM // CHUNK)], axis=0)

    ya = token_major(yat_ref) * _silu(proj(0))
    merged = gate(0) * _dot(ya.astype(BF16), wa_ref[...])
    yb = token_major(hmt_ref) * _sigmoid(proj(1)) * _silu(proj(2))
    merged = merged + gate(1) * _dot(yb.astype(BF16), wb_ref[...])
    yc = _rms(ys_ref[...] * _silu(proj(3)), sg_ref[...])
    merged = merged + gate(2) * _dot(yc.astype(BF16), wc_ref[...])
    o_ref[...] = x + _dot(merged.astype(BF16), wo_ref[...])


def _final_call(x2, ya_t, hm_t, ys, ng, sg, w2, wa, wb, wc, wo):
    t = x2.shape[0]
    tm = FINAL_TM
    row = lambda width: pl.BlockSpec((tm, width), lambda i: (i, 0))
    chunked = pl.BlockSpec((tm // CHUNK, BRANCH, CHUNK), lambda i: (i, 0, 0))
    full = lambda shape: pl.BlockSpec(shape, lambda i: (0, 0))
    return pl.pallas_call(
        _final_kernel,
        grid=(t // tm,),
        in_specs=[
            row(D_MODEL), chunked, chunked, row(BRANCH),
            full((1, D_MODEL)), full((1, BRANCH)),
            full((D_MODEL, W2_WIDTH)), full((BRANCH, D_MODEL)), full((BRANCH, D_MODEL)),
            full((BRANCH, D_MODEL)), full((D_MODEL, D_MODEL)),
        ],
        out_specs=row(D_MODEL),
        out_shape=jax.ShapeDtypeStruct((t, D_MODEL), F32),
        compiler_params=_params(1),
        name="final",
    )(x2, ya_t, hm_t, ys, ng, sg, w2, wa, wb, wc, wo)


_IN_SPLITS = (512, 128, 128, 512, 512, 512, 512, 512, 8, 8, 512, 768, 16, 512, 3072)
_NAMES = ("aq", "ak", "av", "az", "mq", "mk", "mv", "mo", "mi", "mf", "mz", "sxbc", "sdt", "sz", "gates")


def _split_w_in(w):
    bounds = np.cumsum((0,) + _IN_SPLITS)
    return {n: w[:, bounds[i]:bounds[i + 1]] for i, n in enumerate(_NAMES)}


def _layer_params(w_in, q_norm_g, k_norm_g, mlstm_i_b, mlstm_f_b, mlstm_norm_g, a_log, dt_bias, conv_w,
                  conv_b, d_skip):
    p = _split_w_in(w_in)
    zc = lambda n: jnp.zeros((D_MODEL, n), F32)
    ssd_gates = []
    for d in range(2):
        ssd_gates += [zc(SSM_HEADS), p["sdt"][:, 8 * d:8 * d + 8], zc(LANES - 2 * SSM_HEADS)]
    w_row = jnp.concatenate([p["mq"], p["mk"], p["sxbc"], p["ak"]] + ssd_gates, axis=1).astype(BF16)
    w_t = jnp.concatenate([p["mv"], p["aq"], p["av"]], axis=1).T.astype(BF16)
    m_gates = []
    for d in range(2):
        m_gates += [p["mi"][:, 4 * d:4 * d + 4], zc(4), p["mf"][:, 4 * d:4 * d + 4], zc(4)]
    w_g = jnp.concatenate(m_gates, axis=1).T.astype(BF16)
    w2 = jnp.concatenate([p["az"], p["mo"], p["mz"], p["sz"], p["gates"]], axis=1).astype(BF16)
    sdirp = jnp.zeros((2, SUBLANES, LANES), F32)
    sdirp = sdirp.at[:, 0, 8:16].set(dt_bias).at[:, 1, 8:16].set(a_log)
    mdirp = jnp.zeros((2, 2 * SUBLANES, LANES), F32)
    mdirp = mdirp.at[:, 0:4, :].set(mlstm_i_b[:, :, None]).at[:, 8:12, :].set(mlstm_f_b[:, :, None])
    return dict(
        w_row=w_row, w_t=w_t, w_g=w_g, w2=w2, sdirp=sdirp, mdirp=mdirp,
        qg_t=jnp.broadcast_to(q_norm_g[:, None], (ATT_HEAD_DIM, LANES)),
        kg=jnp.tile(k_norm_g, 2)[None, :],
        ng_t=jnp.broadcast_to(mlstm_norm_g[:, None], (BRANCH, LANES)),
        conv_w8=jnp.zeros((SUBLANES, XBC_W), F32).at[:SSM_CONV].set(conv_w), conv_b=conv_b[None, :],
        dskip=jnp.repeat(d_skip, 64)[None, :])


def _rope_tables(s_len):
    inv_freq = ROPE_THETA ** (-jnp.arange(ROT_HALF, dtype=F32) * 2.0 / (2 * ROT_HALF))
    ang = jnp.arange(s_len).astype(F32)[:, None] * inv_freq
    cos, sin = jnp.cos(ang), jnp.sin(ang)
    rest = ATT_HEAD_DIM - 2 * ROT_HALF
    one = jnp.ones((s_len, rest), F32)
    zer = jnp.zeros((s_len, rest), F32)
    z8 = jnp.zeros((s_len, ROT_HALF), F32)
    tc = jnp.concatenate([cos, cos, one], axis=1)
    t1 = jnp.concatenate([-sin, z8, zer], axis=1)
    t2 = jnp.concatenate([z8, sin, zer], axis=1)
    rows = jnp.concatenate([jnp.tile(t, (1, 2)) for t in (tc, t1, t2)], axis=1)
    nb = s_len // CHUNK
    t_tab = jnp.concatenate([cos.T.reshape(ROT_HALF, nb, CHUNK), sin.T.reshape(ROT_HALF, nb, CHUNK)], axis=0)
    return rows, t_tab.transpose(1, 0, 2)


def _block_diag_mean(width):
    blk = np.kron(np.eye(width // ATT_HEAD_DIM), np.full((ATT_HEAD_DIM, ATT_HEAD_DIM), 1.0 / ATT_HEAD_DIM))
    return jnp.asarray(blk, BF16)


def _layer(x, lp, ropes, bdk, norm_g, attn_sink, ssm_norm_g, wa, wb, wc, wo):
    b, s, _ = x.shape
    assert s % (CHUNK * MLSTM_BLOCK) == 0 and s >= 2 * CHUNK and (b * s) % PROJ_TM == 0, (b, s)
    x2 = x.reshape(b * s, D_MODEL)
    mqk, xbc, small, p_t, g_t = _proj_call(x2, norm_g[None, :], lp["w_row"], lp["w_t"], lp["w_g"])
    ya_t = _att_call(small, p_t, b, s, ropes[0], ropes[1], lp["qg_t"], lp["kg"], bdk, attn_sink)
    hm_fwd = _mlstm_call(False, mqk, p_t, g_t, b, s, lp["mdirp"])
    hm_t = _mlstm_call(True, mqk, p_t, g_t, b, s, lp["mdirp"], hm_fwd, lp["ng_t"])
    ys = _ssd_calls(xbc, small, b, s, lp["sdirp"], lp["conv_w8"], lp["conv_b"], lp["dskip"])
    out = _final_call(x2, ya_t, hm_t, ys.reshape(b * s, BRANCH), norm_g[None, :], ssm_norm_g[None, :],
                      lp["w2"], wa, wb, wc, wo)
    return out.reshape(b, s, D_MODEL)


def kernel(x_prompt, x_sample, norm_g, w_in, q_norm_g, k_norm_g, attn_sink, w_att_out, mlstm_i_b, mlstm_f_b,
           mlstm_norm_g, w_mlstm_out, conv_w, conv_b, a_log, dt_bias, d_skip, ssm_norm_g, w_ssm_out, w_out):
    depth = w_in.shape[0]
    bdk = _block_diag_mean(LANES)
    ropes = {x.shape[1]: _rope_tables(x.shape[1]) for x in (x_prompt, x_sample)}
    ys = [x_prompt, x_sample]
    for l in range(depth):
        lp = _layer_params(w_in[l], q_norm_g[l], k_norm_g[l], mlstm_i_b[l], mlstm_f_b[l], mlstm_norm_g[l],
                           a_log[l], dt_bias[l], conv_w[l], conv_b[l], d_skip[l])
        wa, wb, wc, wo = (w.astype(BF16) for w in (w_att_out[l], w_mlstm_out[l], w_ssm_out[l], w_out[l]))
        ys = [_layer(y, lp, ropes[y.shape[1]], bdk, norm_g[l], attn_sink[l], ssm_norm_g[l], wa, wb, wc, wo)
              for y in ys]
    return tuple(ys)
```

```python
import functools

import jax
import jax.numpy as jnp
import numpy as np
from jax import lax
from jax.experimental import pallas as pl
from jax.experimental.pallas import tpu as pltpu

F32 = jnp.float32
BF16 = jnp.bfloat16

D_MODEL = 1024
BRANCH = 512
RMS_EPS = 1e-6
CHUNK = 128
LANES = 128
SUBLANES = 8
BF16_ROWS = 16
ATT_HEADS = 8
ATT_HEAD_DIM = 64
ROT_HALF = 8
ROPE_THETA = 500000.0
MLSTM_HEADS = 4
MLSTM_DK = 128
MLSTM_BLOCK = 4
SSM_HEADS = 8
SSM_CONV = 5
HALO = BF16_ROWS
NEG = -1e30

MQK_W = 1024
XBC_W = 768
SMALL_W = 384
ROW_W = MQK_W + XBC_W + SMALL_W
T_MV, T_AQ, T_AV = 0, 512, 1024
T_ROWS = 1152
G_ROWS = 32
W2_WIDTH = 5120
PROJ_TM = 512
FINAL_TM = 512

VMEM_LIMIT = 56 * 1024 * 1024


def _dot(a, b):
    return jnp.dot(a, b, preferred_element_type=F32)


def _dot_nt(a, b):
    return lax.dot_general(a, b, (((1,), (1,)), ((), ())), preferred_element_type=F32)


def _split3(x):
    h1 = x.astype(BF16)
    r1 = x - h1.astype(F32)
    h2 = r1.astype(BF16)
    r2 = r1 - h2.astype(F32)
    return h1, h2, r2.astype(BF16)


def _cum_left(mask_bf, x):
    h1, h2, h3 = _split3(x)
    return _dot(mask_bf, h1) + _dot(mask_bf, h2) + _dot(mask_bf, h3)


def _cum_right(x, mask_bf):
    h1, h2, h3 = _split3(x)
    return _dot(h1, mask_bf) + _dot(h2, mask_bf) + _dot(h3, mask_bf)


def _sigmoid(x):
    return 1.0 / (1.0 + jnp.exp(-x))


def _silu(x):
    return x * _sigmoid(x)


def _softplus(x):
    return jnp.maximum(x, 0.0) + jnp.log1p(jnp.exp(-jnp.abs(x)))


def _log_sigmoid(x):
    return -_softplus(-x)


def _rms(x, g_row):
    ms = jnp.mean(x * x, axis=-1, keepdims=True)
    return x * lax.rsqrt(ms + RMS_EPS) * g_row


def _params(n_axes):
    return pltpu.CompilerParams(dimension_semantics=("arbitrary",) * n_axes, vmem_limit_bytes=VMEM_LIMIT)


def _proj_kernel(x_ref, ng_ref, wr_ref, wt_ref, wg_ref, mqk_ref, xbc_ref, small_ref, t_ref, g_ref):
    h = _rms(x_ref[...], ng_ref[...]).astype(BF16)
    for j0 in range(0, MQK_W, 512):
        mqk_ref[:, j0:j0 + 512] = _dot(h, wr_ref[:, j0:j0 + 512]).astype(BF16)
    xbc_ref[...] = _dot(h, wr_ref[:, MQK_W:MQK_W + XBC_W]).astype(BF16)
    small_ref[...] = _dot(h, wr_ref[:, MQK_W + XBC_W:])
    n_chunks = PROJ_TM // CHUNK
    for r0 in range(0, T_ROWS, 384):
        res = _dot_nt(wt_ref[r0:r0 + 384, :], h).astype(BF16)
        for j in range(n_chunks):
            t_ref[j, r0:r0 + 384, :] = res[:, j * CHUNK:(j + 1) * CHUNK]
    gates = _dot_nt(wg_ref[...], h)
    for j in range(n_chunks):
        g_ref[j] = gates[:, j * CHUNK:(j + 1) * CHUNK]


def _proj_call(x2, ng, w_row, w_t, w_g):
    t = x2.shape[0]
    tm = PROJ_TM
    full = lambda shape: pl.BlockSpec(shape, lambda i: (0, 0))
    rows = lambda width: pl.BlockSpec((tm, width), lambda i: (i, 0))
    chunks = lambda r: pl.BlockSpec((tm // CHUNK, r, CHUNK), lambda i: (i, 0, 0))
    return pl.pallas_call(
        _proj_kernel,
        grid=(t // tm,),
        in_specs=[rows(D_MODEL), full((1, D_MODEL)), full((D_MODEL, ROW_W)), full((T_ROWS, D_MODEL)),
                  full((G_ROWS, D_MODEL))],
        out_specs=[rows(MQK_W), rows(XBC_W), rows(SMALL_W), chunks(T_ROWS), chunks(G_ROWS)],
        out_shape=[
            jax.ShapeDtypeStruct((t, MQK_W), BF16),
            jax.ShapeDtypeStruct((t, XBC_W), BF16),
            jax.ShapeDtypeStruct((t, SMALL_W), F32),
            jax.ShapeDtypeStruct((t // CHUNK, T_ROWS, CHUNK), BF16),
            jax.ShapeDtypeStruct((t // CHUNK, G_ROWS, CHUNK), F32),
        ],
        compiler_params=_params(1),
        name="proj",
    )(x2, ng, w_row, w_t, w_g)


def _rope_rows(x, tab):
    w = x.shape[1]
    return (x * tab[:, :w]
            + pltpu.roll(x, w - ROT_HALF, 1) * tab[:, w:2 * w]
            + pltpu.roll(x, ROT_HALF, 1) * tab[:, 2 * w:])


ATT_QB = 4


def _att_kernel(kp_ref, kc_ref, kn_ref, qt_ref, vp_ref, vc_ref, vn_ref, rp_ref, rc_ref, rn_ref,
                rt_ref, qg_ref, kg_ref, bdk_ref, sink_ref, o_ref):
    i = pl.program_id(1)
    ni = pl.num_programs(1)
    win = 3 * CHUNK
    nkeys = (ATT_QB + 2) * CHUNK

    k = jnp.concatenate([kp_ref[...], kc_ref[...], kn_ref[...]], axis=0)
    tab_k = jnp.concatenate([rp_ref[...], rc_ref[...], rn_ref[...]], axis=0)
    kms = _dot((k * k).astype(BF16), bdk_ref[...])
    kr = _rope_rows(k * lax.rsqrt(kms + RMS_EPS) * kg_ref[...], tab_k)
    lane = lax.broadcasted_iota(jnp.int32, (nkeys, LANES), 1)
    lo = lane < ATT_HEAD_DIM
    k_sw = pltpu.roll(kr, ATT_HEAD_DIM, 1)
    zero = jnp.zeros_like(kr)
    k_var = [[jnp.where(lo, kr, zero).astype(BF16), jnp.where(lo, zero, k_sw).astype(BF16)],
             [jnp.where(lo, k_sw, zero).astype(BF16), jnp.where(lo, zero, kr).astype(BF16)]]

    v_t = jnp.concatenate([vp_ref[...]] + [vc_ref[j] for j in range(ATT_QB)] + [vn_ref[...]], axis=1)

    s_idx = lax.broadcasted_iota(jnp.int32, (win, 2 * CHUNK), 0)
    t_idx = lax.broadcasted_iota(jnp.int32, (win, 2 * CHUNK), 1) % CHUNK
    band = (s_idx >= t_idx) & (s_idx <= t_idx + 2 * CHUNK)
    first_head = lax.broadcasted_iota(jnp.int32, (1, 2 * CHUNK), 1) < CHUNK

    s2 = []
    for qb in range(ATT_QB):
        cos = rt_ref[qb, 0:ROT_HALF, :]
        sin = rt_ref[qb, ROT_HALF:, :]
        q_heads = []
        for h in range(ATT_HEADS):
            x = qt_ref[qb, h * ATT_HEAD_DIM:(h + 1) * ATT_HEAD_DIM, :].astype(F32)
            ms = jnp.mean(x * x, axis=0, keepdims=True)
            xn = x * lax.rsqrt(ms + RMS_EPS) * qg_ref[...]
            x1 = xn[0:ROT_HALF]
            x2 = xn[ROT_HALF:2 * ROT_HALF]
            xr = jnp.concatenate([x1 * cos - x2 * sin, x2 * cos + x1 * sin, xn[2 * ROT_HALF:]], axis=0)
            q_heads.append((xr * (ATT_HEAD_DIM ** -0.5)).astype(BF16))
        keys = slice(qb * CHUNK, qb * CHUNK + win)
        for g in range(2):
            lhs = jnp.concatenate([k_var[g][0][keys], k_var[g][1][keys]], axis=0)
            rhs = jnp.concatenate(
                [jnp.concatenate([q_heads[4 * g + 2 * j], q_heads[4 * g + 2 * j + 1]], axis=0)
                 for j in range(2)], axis=1)
            s2.append(_dot(lhs, rhs))
    ps, dens = [], []
    for qb in range(ATT_QB):
        valid = band
        if qb == 0:
            valid = valid & ((s_idx >= CHUNK) | (i > 0))
        if qb == ATT_QB - 1:
            valid = valid & ((s_idx < 2 * CHUNK) | (i < ni - 1))
        bias = jnp.where(valid, 0.0, NEG)
        for g in range(2):
            for half in range(2):
                s = s2[2 * qb + g][half * win:(half + 1) * win] + bias
                sink = jnp.where(first_head, sink_ref[4 * g + half], sink_ref[4 * g + 2 + half])
                mx = jnp.maximum(jnp.max(s, axis=0, keepdims=True), sink)
                p = jnp.exp(s - mx)
                dens.append(jnp.sum(p, axis=0, keepdims=True) + jnp.exp(sink - mx))
                ps.append(p.astype(BF16))
    for qb in range(ATT_QB):
        outs = [None] * ATT_HEADS
        for g in range(2):
            vg = v_t[g * ATT_HEAD_DIM:(g + 1) * ATT_HEAD_DIM, qb * CHUNK:qb * CHUNK + win]
            for half in range(2):
                n = 4 * qb + 2 * g + half
                o = _dot(vg, ps[n]) * (1.0 / dens[n])
                outs[4 * g + half] = o[:, :CHUNK]
                outs[4 * g + 2 + half] = o[:, CHUNK:]
        o_ref[qb] = jnp.concatenate(outs, axis=0)


def _att_call(small, p_t, b, s, rope_rows, rope_t, qg_t, kg, bdk, sink):
    nb = s // CHUNK
    ni = nb // ATT_QB
    small3 = small.reshape(b, s, SMALL_W)
    prev = lambda i: jnp.maximum(ATT_QB * i - 1, 0)
    nxt = lambda i: jnp.minimum(ATT_QB * i + ATT_QB, nb - 1)
    av = T_AV // LANES
    return pl.pallas_call(
        _att_kernel,
        grid=(b, ni),
        in_specs=[
            pl.BlockSpec((None, CHUNK, LANES), lambda bi, i: (bi, prev(i), 0)),
            pl.BlockSpec((None, ATT_QB * CHUNK, LANES), lambda bi, i: (bi, i, 0)),
            pl.BlockSpec((None, CHUNK, LANES), lambda bi, i: (bi, nxt(i), 0)),
            pl.BlockSpec((ATT_QB, BRANCH, CHUNK), lambda bi, i: (bi * ni + i, T_AQ // BRANCH, 0)),
            pl.BlockSpec((None, LANES, CHUNK), lambda bi, i: (bi * nb + prev(i), av, 0)),
            pl.BlockSpec((ATT_QB, LANES, CHUNK), lambda bi, i: (bi * ni + i, av, 0)),
            pl.BlockSpec((None, LANES, CHUNK), lambda bi, i: (bi * nb + nxt(i), av, 0)),
            pl.BlockSpec((CHUNK, 3 * LANES), lambda bi, i: (prev(i), 0)),
            pl.BlockSpec((ATT_QB * CHUNK, 3 * LANES), lambda bi, i: (i, 0)),
            pl.BlockSpec((CHUNK, 3 * LANES), lambda bi, i: (nxt(i), 0)),
            pl.BlockSpec((ATT_QB, 2 * ROT_HALF, CHUNK), lambda bi, i: (i, 0, 0)),
            pl.BlockSpec((ATT_HEAD_DIM, LANES), lambda bi, i: (0, 0)),
            pl.BlockSpec((1, LANES), lambda bi, i: (0, 0)),
            pl.BlockSpec((LANES, LANES), lambda bi, i: (0, 0)),
            pl.BlockSpec(memory_space=pltpu.SMEM),
        ],
        out_specs=pl.BlockSpec((ATT_QB, BRANCH, CHUNK), lambda bi, i: (bi * ni + i, 0, 0)),
        out_shape=jax.ShapeDtypeStruct((b * nb, BRANCH, CHUNK), F32),
        compiler_params=_params(2),
        name="att",
    )(small3, small3, small3, p_t, p_t, p_t, p_t, rope_rows, rope_rows, rope_rows, rope_t,
      qg_t, kg, bdk, sink)


def _scan_masks(reverse):
    r = lax.broadcasted_iota(jnp.int32, (CHUNK, CHUNK), 0)
    c = lax.broadcasted_iota(jnp.int32, (CHUNK, CHUNK), 1)
    causal = (r <= c) if reverse else (r >= c)
    anti = (r >= c) if reverse else (r <= c)
    return causal, anti


def _scan_block(reverse, n):
    return (lambda c: n - 1 - c) if reverse else (lambda c: c)


AUG = MLSTM_DK + BF16_ROWS
LOG_QK_SCALE = float(np.log(MLSTM_DK ** -0.5))


def _mlstm_kernel(reverse, qk_ref, vt_ref, g_ref, dp_ref, *rest):
    if reverse:
        prev_ref, ng_ref, o_ref, st_sc, m_sc = rest
    else:
        o_ref, st_sc, m_sc = rest
    nblk = MLSTM_BLOCK
    rows = nblk * SUBLANES

    @pl.when(pl.program_id(1) == 0)
    def _():
        st_sc[...] = jnp.zeros_like(st_sc)
        m_sc[...] = jnp.zeros_like(m_sc)

    _, anti = _scan_masks(reverse)
    g = g_ref[...]
    ig = (g[:, 0:SUBLANES, :] + dp_ref[0:SUBLANES, :]).reshape(rows, CHUNK)
    lf = _log_sigmoid(g[:, SUBLANES:, :] + dp_ref[SUBLANES:, :]).reshape(rows, CHUNK)
    b = _cum_right(lf, anti.astype(BF16))
    a = ig - b
    a_max = jnp.max(a, axis=1, keepdims=True)
    b_tot = jnp.sum(lf, axis=1, keepdims=True)
    a_col = jnp.concatenate([a, jnp.zeros((CHUNK - rows, CHUNK), F32)], axis=0).T
    ones = jnp.ones((AUG - MLSTM_DK, CHUNK), BF16)
    scale = MLSTM_DK ** -0.5

    m_st = m_sc[...][:, 0:1]
    st = [st_sc[h] for h in range(MLSTM_HEADS)]
    order = range(nblk - 1, -1, -1) if reverse else range(nblk)
    for j in order:
        sl = slice(j * SUBLANES, (j + 1) * SUBLANES)
        tok = slice(j * CHUNK, (j + 1) * CHUNK)
        mm = jnp.maximum(m_st, a_max[sl])
        wk = jnp.exp(a[sl] - mm) * scale
        decay = jnp.exp(m_st - mm)
        ks, vts, sts = [], [], []
        for h in range(MLSTM_HEADS):
            q = qk_ref[tok, h * MLSTM_DK:(h + 1) * MLSTM_DK]
            k = qk_ref[tok, BRANCH + h * MLSTM_DK:BRANCH + (h + 1) * MLSTM_DK]
            ks.append(k)
            vts.append(jnp.concatenate([vt_ref[j, h * MLSTM_DK:(h + 1) * MLSTM_DK, :], ones], axis=0))
            sts.append(_dot_nt(jnp.concatenate([k, st[h].astype(BF16)], axis=0), q))
        scs, w_inters, m_ts = [], [], []
        for h in range(MLSTM_HEADS):
            col = j * SUBLANES + h
            a_m = jnp.where(anti, jnp.broadcast_to(a_col[:, col:col + 1], (CHUNK, CHUNK)), NEG)
            m_h = m_st[h:h + 1, :]
            u = jnp.maximum(m_h, jnp.max(a_m, axis=0, keepdims=True))
            m_ts.append(b[col:col + 1, :] + u)
            w_inters.append(jnp.exp(m_h - u))
            scs.append((sts[h][:CHUNK] * jnp.exp(a_m - (u - LOG_QK_SCALE))).astype(BF16))
        outs = []
        for h in range(MLSTM_HEADS):
            nd = _dot(vts[h], scs[h]) + w_inters[h] * sts[h][CHUNK:]
            den = jnp.maximum(jnp.abs(nd[MLSTM_DK:MLSTM_DK + 1, :]), jnp.exp(-m_ts[h]))
            outs.append(nd[:MLSTM_DK] * (1.0 / den))
        for h in range(MLSTM_HEADS):
            vw = (vts[h].astype(F32) * wk[h:h + 1, :]).astype(BF16)
            st[h] = decay[h:h + 1, :] * st[h] + _dot(vw, ks[h])
        m_st = b_tot[sl] + mm
        if reverse:
            normed = []
            for h in range(MLSTM_HEADS):
                x = prev_ref[j, h * MLSTM_DK:(h + 1) * MLSTM_DK, :] + outs[h]
                ms = jnp.mean(x * x, axis=0, keepdims=True)
                normed.append(x * lax.rsqrt(ms + RMS_EPS))
            o_ref[j] = jnp.concatenate(normed, axis=0) * ng_ref[...]
        else:
            o_ref[j] = jnp.concatenate(outs, axis=0)
    for h in range(MLSTM_HEADS):
        st_sc[h] = st[h]
    m_sc[...] = jnp.broadcast_to(m_st, (SUBLANES, LANES))


def _mlstm_call(reverse, mqk, p_t, g_t, b, s, dirp, prev=None, ng_t=None):
    nb = s // (CHUNK * MLSTM_BLOCK)
    blk = _scan_block(reverse, nb)
    mqk3 = mqk.reshape(b, s, MQK_W)
    d = int(reverse)
    out_spec = pl.BlockSpec((MLSTM_BLOCK, BRANCH, CHUNK), lambda bi, c: (bi * nb + blk(c), 0, 0))
    in_specs = [
        pl.BlockSpec((None, CHUNK * MLSTM_BLOCK, MQK_W), lambda bi, c: (bi, blk(c), 0)),
        pl.BlockSpec((MLSTM_BLOCK, BRANCH, CHUNK), lambda bi, c: (bi * nb + blk(c), T_MV // BRANCH, 0)),
        pl.BlockSpec((MLSTM_BLOCK, 2 * SUBLANES, CHUNK), lambda bi, c: (bi * nb + blk(c), d, 0)),
        pl.BlockSpec((None, 2 * SUBLANES, LANES), lambda bi, c: (d, 0, 0)),
    ]
    args = [mqk3, p_t, g_t, dirp]
    if reverse:
        in_specs += [out_spec, pl.BlockSpec((BRANCH, LANES), lambda bi, c: (0, 0))]
        args += [prev, ng_t]
    return pl.pallas_call(
        functools.partial(_mlstm_kernel, reverse),
        grid=(b, nb),
        in_specs=in_specs,
        out_specs=out_spec,
        out_shape=jax.ShapeDtypeStruct((b * s // CHUNK, BRANCH, CHUNK), F32),
        scratch_shapes=[
            pltpu.VMEM((MLSTM_HEADS, AUG, MLSTM_DK), F32),
            pltpu.VMEM((SUBLANES, LANES), F32),
        ],
        compiler_params=_params(2),
        name="mlstm_bwd" if reverse else "mlstm_fwd",
    )(*args)


SSD_BLOCK = 4
LOG2E = float(np.log2(np.e))


def _ssd_local(reverse, xs_b, bt, cm_b, g, dp_ref):
    causal, anti = _scan_masks(reverse)
    dt = _softplus(g + dp_ref[0:1, :])
    da = dt * (-jnp.exp(dp_ref[1:2, :]) * LOG2E)
    da_t = da.T
    acs_c = _cum_left(causal.astype(BF16), da)
    acs_r = _cum_right(da_t, anti.astype(BF16)) - jnp.log2(dt.T)
    tot = jnp.sum(da_t, axis=1, keepdims=True)

    lane = lax.broadcasted_iota(jnp.int32, (CHUNK, LANES), 1)
    lo = lane < 64
    lo_row = lo[0:1, :]
    bt_b = bt.astype(BF16)
    zero_b = jnp.zeros((CHUNK, LANES), BF16)
    cb = [_dot(jnp.where(lo, cm_b, zero_b), bt_b), _dot(jnp.where(lo, zero_b, cm_b), bt_b)]

    out = []
    for j in range(4):
        grp = j // 2
        m_parts, bw_parts, eac, edec = [], [], [], []
        for e in (2 * j, 2 * j + 1):
            ln = SSM_HEADS + e
            ac = acs_c[:, ln:ln + 1]
            ar = acs_r[ln:ln + 1, :]
            te = tot[ln:ln + 1, :]
            m_parts.append((cb[grp] * jnp.exp2(jnp.where(causal, ac - ar, NEG))).astype(BF16))
            bw_parts.append((bt * jnp.exp2(te - ar)).astype(BF16))
            eac.append(jnp.exp2(ac))
            edec.append(jnp.exp2(te))
        lhs = jnp.concatenate([jnp.concatenate(m_parts, axis=1),
                               jnp.concatenate(bw_parts, axis=1)], axis=0)
        xp = xs_b[:, j * LANES:(j + 1) * LANES]
        x2 = jnp.concatenate([jnp.where(lo, xp, zero_b), jnp.where(lo, zero_b, xp)], axis=0)
        out.append((_dot(lhs, x2), jnp.where(lo, eac[0], eac[1]), jnp.where(lo_row, edec[0], edec[1])))
    return out


def _ssd_carry(local, cm_b, st):
    row = lax.broadcasted_iota(jnp.int32, (CHUNK, LANES), 0)
    ys = []
    for j in range(4):
        res, eac, edec = local[j]
        grp = j // 2
        in_grp = (row >= 64 * grp) & (row < 64 * grp + 64)
        ys.append(res[:CHUNK] + _dot(cm_b, st[j].astype(BF16)) * eac)
        st[j] = st[j] * edec + jnp.where(in_grp, res[CHUNK:], 0.0)
    return jnp.concatenate(ys, axis=1)


def _ssd_fwd_kernel(x_ref, xp_ref, xn_ref, g_ref, dp_ref, cw_ref, cb_ref, dsk_ref, y_ref, xc_ref,
                    st_sc, pad_sc):
    c = pl.program_id(1)
    nc = pl.num_programs(1)
    rows = SSD_BLOCK * CHUNK

    @pl.when(c == 0)
    def _():
        st_sc[...] = jnp.zeros_like(st_sc)

    pad_sc[0:HALO, :] = jnp.where(c > 0, xp_ref[...].astype(F32), 0.0)
    pad_sc[HALO:HALO + rows, :] = x_ref[...].astype(F32)
    pad_sc[HALO + rows:, :] = jnp.where(c < nc - 1, xn_ref[...].astype(F32), 0.0)
    xcs, local = [], []
    for j in range(SSD_BLOCK):
        tok = slice(j * CHUNK, (j + 1) * CHUNK)
        acc = jnp.broadcast_to(cb_ref[...], (CHUNK, XBC_W))
        for t in range(SSM_CONV):
            off = HALO - SSM_CONV // 2 + t + j * CHUNK
            acc = acc + pad_sc[off:off + CHUNK, :] * cw_ref[t:t + 1, :]
        xc = _silu(acc)
        xc_b = xc.astype(BF16)
        xc_ref[tok, :] = xc_b
        xcs.append((xc, xc_b))
        local.append(_ssd_local(False, xc_b[:, :BRANCH], xc[:, BRANCH:BRANCH + LANES].T,
                                xc_b[:, BRANCH + LANES:], g_ref[tok, :], dp_ref))
    st = [st_sc[j] for j in range(4)]
    for j in range(SSD_BLOCK):
        xc, xc_b = xcs[j]
        y = _ssd_carry(local[j], xc_b[:, BRANCH + LANES:], st)
        y_ref[j * CHUNK:(j + 1) * CHUNK, :] = y + dsk_ref[...] * xc[:, :BRANCH]
    for j in range(4):
        st_sc[j] = st[j]


def _ssd_bwd_kernel(xc_ref, g_ref, dp_ref, prev_ref, y_ref, st_sc):
    @pl.when(pl.program_id(1) == 0)
    def _():
        st_sc[...] = jnp.zeros_like(st_sc)

    order = range(SSD_BLOCK - 1, -1, -1)
    local = {}
    for j in order:
        tok = slice(j * CHUNK, (j + 1) * CHUNK)
        bt = xc_ref[tok, BRANCH:BRANCH + LANES].astype(F32).T
        local[j] = _ssd_local(True, xc_ref[tok, :BRANCH], bt, xc_ref[tok, BRANCH + LANES:],
                              g_ref[tok, :], dp_ref)
    st = [st_sc[j] for j in range(4)]
    for j in order:
        tok = slice(j * CHUNK, (j + 1) * CHUNK)
        y_ref[tok, :] = prev_ref[tok, :] + _ssd_carry(local[j], xc_ref[tok, BRANCH + LANES:], st)
    for j in range(4):
        st_sc[j] = st[j]


def _ssd_calls(xbc, small, b, s, dirp, conv_w8, conv_b, dskip):
    rows = SSD_BLOCK * CHUNK
    nb = s // rows
    per = rows // HALO
    last = s // HALO - 1
    xbc3 = xbc.reshape(b, s, XBC_W)
    small3 = small.reshape(b, s, SMALL_W)
    state = pltpu.VMEM((4, CHUNK, LANES), F32)
    y_shape = jax.ShapeDtypeStruct((b, s, BRANCH), F32)
    const = lambda shape: pl.BlockSpec(shape, lambda bi, c: (0,) * len(shape))

    def block_spec(width, blk, col=0):
        return pl.BlockSpec((None, rows, width), lambda bi, c: (bi, blk(c), col))

    fwd = _scan_block(False, nb)
    y_fwd, xc = pl.pallas_call(
        _ssd_fwd_kernel,
        grid=(b, nb),
        in_specs=[
            block_spec(XBC_W, fwd),
            pl.BlockSpec((None, HALO, XBC_W), lambda bi, c: (bi, jnp.maximum(c * per - 1, 0), 0)),
            pl.BlockSpec((None, HALO, XBC_W), lambda bi, c: (bi, jnp.minimum(c * per + per, last), 0)),
            block_spec(LANES, fwd, 1),
            pl.BlockSpec((None, SUBLANES, LANES), lambda bi, c: (0, 0, 0)),
            const((SUBLANES, XBC_W)), const((1, XBC_W)), const((1, BRANCH)),
        ],
        out_specs=[block_spec(BRANCH, fwd), block_spec(XBC_W, fwd)],
        out_shape=[y_shape, jax.ShapeDtypeStruct((b, s, XBC_W), BF16)],
        scratch_shapes=[state, pltpu.VMEM((rows + 2 * HALO, XBC_W), F32)],
        compiler_params=_params(2),
        name="ssd_fwd",
    )(xbc3, xbc3, xbc3, small3, dirp, conv_w8, conv_b, dskip)

    bwd = _scan_block(True, nb)
    return pl.pallas_call(
        _ssd_bwd_kernel,
        grid=(b, nb),
        in_specs=[
            block_spec(XBC_W, bwd),
            block_spec(LANES, bwd, 2),
            pl.BlockSpec((None, SUBLANES, LANES), lambda bi, c: (1, 0, 0)),
            block_spec(BRANCH, bwd),
        ],
        out_specs=block_spec(BRANCH, bwd),
        out_shape=y_shape,
        scratch_shapes=[state],
        compiler_params=_params(2),
        name="ssd_bwd",
    )(xc, small3, dirp, y_fwd)


def _final_kernel(x_ref, yat_ref, hmt_ref, ys_ref, ng_ref, sg_ref, w2_ref, wa_ref, wb_ref, wc_ref,
                  wo_ref, o_ref):
    x = x_ref[...]
    h = _rms(x, ng_ref[...]).astype(BF16)

    def proj(k):
        return _dot(h, w2_ref[:, k * BRANCH:(k + 1) * BRANCH])

    def gate(k):
        lo = 4 * BRANCH + k * D_MODEL
        return _sigmoid(_dot(h, w2_ref[:, lo:lo + D_MODEL]))

    def token_major(ref):
        return jnp.concatenate([ref[j].T for j in range(FINAL_TM // CHUNK)], axis=0)

    ya = token_major(yat_ref) * _silu(proj(0))
    merged = gate(0) * _dot(ya.astype(BF16), wa_ref[...])
    yb = token_major(hmt_ref) * _sigmoid(proj(1)) * _silu(proj(2))
    merged = merged + gate(1) * _dot(yb.astype(BF16), wb_ref[...])
    yc = _rms(ys_ref[...] * _silu(proj(3)), sg_ref[...])
    merged = merged + gate(2) * _dot(yc.astype(BF16), wc_ref[...])
    o_ref[...] = x + _dot(merged.astype(BF16), wo_ref[...])


def _final_call(x2, ya_t, hm_t, ys, ng, sg, w2, wa, wb, wc, wo):
    t = x2.shape[0]
    tm = FINAL_TM
    row = lambda width: pl.BlockSpec((tm, width), lambda i: (i, 0))
    chunked = pl.BlockSpec((tm // CHUNK, BRANCH, CHUNK), lambda i: (i, 0, 0))
    full = lambda shape: pl.BlockSpec(shape, lambda i: (0, 0), pipeline_mode=pl.Buffered(1))
    return pl.pallas_call(
        _final_kernel,
        grid=(t // tm,),
        in_specs=[
            row(D_MODEL), chunked, chunked, row(BRANCH),
            full((1, D_MODEL)), full((1, BRANCH)),
            full((D_MODEL, W2_WIDTH)), full((BRANCH, D_MODEL)), full((BRANCH, D_MODEL)),
            full((BRANCH, D_MODEL)), full((D_MODEL, D_MODEL)),
        ],
        out_specs=row(D_MODEL),
        out_shape=jax.ShapeDtypeStruct((t, D_MODEL), F32),
        compiler_params=_params(1),
        name="final",
    )(x2, ya_t, hm_t, ys, ng, sg, w2, wa, wb, wc, wo)


_IN_SPLITS = (512, 128, 128, 512, 512, 512, 512, 512, 8, 8, 512, 768, 16, 512, 3072)
_NAMES = ("aq", "ak", "av", "az", "mq", "mk", "mv", "mo", "mi", "mf", "mz", "sxbc", "sdt", "sz", "gates")


def _split_w_in(w):
    bounds = np.cumsum((0,) + _IN_SPLITS)
    return {n: w[:, bounds[i]:bounds[i + 1]] for i, n in enumerate(_NAMES)}


def _layer_params(w_in, q_norm_g, k_norm_g, mlstm_i_b, mlstm_f_b, mlstm_norm_g, a_log, dt_bias, conv_w,
                  conv_b, d_skip):
    p = _split_w_in(w_in)
    zc = lambda n: jnp.zeros((D_MODEL, n), F32)
    ssd_gates = []
    for d in range(2):
        ssd_gates += [zc(SSM_HEADS), p["sdt"][:, 8 * d:8 * d + 8], zc(LANES - 2 * SSM_HEADS)]
    w_row = jnp.concatenate([p["mq"], p["mk"], p["sxbc"], p["ak"]] + ssd_gates, axis=1).astype(BF16)
    w_t = jnp.concatenate([p["mv"], p["aq"], p["av"]], axis=1).T.astype(BF16)
    m_gates = []
    for d in range(2):
        m_gates += [p["mi"][:, 4 * d:4 * d + 4], zc(4), p["mf"][:, 4 * d:4 * d + 4], zc(4)]
    w_g = jnp.concatenate(m_gates, axis=1).T.astype(BF16)
    w2 = jnp.concatenate([p["az"], p["mo"], p["mz"], p["sz"], p["gates"]], axis=1).astype(BF16)
    sdirp = jnp.zeros((2, SUBLANES, LANES), F32)
    sdirp = sdirp.at[:, 0, 8:16].set(dt_bias).at[:, 1, 8:16].set(a_log)
    mdirp = jnp.zeros((2, 2 * SUBLANES, LANES), F32)
    mdirp = mdirp.at[:, 0:4, :].set(mlstm_i_b[:, :, None]).at[:, 8:12, :].set(mlstm_f_b[:, :, None])
    return dict(
        w_row=w_row, w_t=w_t, w_g=w_g, w2=w2, sdirp=sdirp, mdirp=mdirp,
        qg_t=jnp.broadcast_to(q_norm_g[:, None], (ATT_HEAD_DIM, LANES)),
        kg=jnp.tile(k_norm_g, 2)[None, :],
        ng_t=jnp.broadcast_to(mlstm_norm_g[:, None], (BRANCH, LANES)),
        conv_w8=jnp.zeros((SUBLANES, XBC_W), F32).at[:SSM_CONV].set(conv_w), conv_b=conv_b[None, :],
        dskip=jnp.repeat(d_skip, 64)[None, :])


def _rope_tables(s_len):
    inv_freq = ROPE_THETA ** (-jnp.arange(ROT_HALF, dtype=F32) * 2.0 / (2 * ROT_HALF))
    ang = jnp.arange(s_len).astype(F32)[:, None] * inv_freq
    cos, sin = jnp.cos(ang), jnp.sin(ang)
    rest = ATT_HEAD_DIM - 2 * ROT_HALF
    one = jnp.ones((s_len, rest), F32)
    zer = jnp.zeros((s_len, rest), F32)
    z8 = jnp.zeros((s_len, ROT_HALF), F32)
    tc = jnp.concatenate([cos, cos, one], axis=1)
    t1 = jnp.concatenate([-sin, z8, zer], axis=1)
    t2 = jnp.concatenate([z8, sin, zer], axis=1)
    rows = jnp.concatenate([jnp.tile(t, (1, 2)) for t in (tc, t1, t2)], axis=1)
    nb = s_len // CHUNK
    t_tab = jnp.concatenate([cos.T.reshape(ROT_HALF, nb, CHUNK), sin.T.reshape(ROT_HALF, nb, CHUNK)], axis=0)
    return rows, t_tab.transpose(1, 0, 2)


def _block_diag_mean(width):
    blk = np.kron(np.eye(width // ATT_HEAD_DIM), np.full((ATT_HEAD_DIM, ATT_HEAD_DIM), 1.0 / ATT_HEAD_DIM))
    return jnp.asarray(blk, BF16)


def _layer(x, lp, ropes, bdk, norm_g, attn_sink, ssm_norm_g, wa, wb, wc, wo):
    b, s, _ = x.shape
    assert s % (CHUNK * max(MLSTM_BLOCK, SSD_BLOCK, ATT_QB)) == 0 and (b * s) % max(PROJ_TM, FINAL_TM) == 0, (b, s)
    x2 = x.reshape(b * s, D_MODEL)
    mqk, xbc, small, p_t, g_t = _proj_call(x2, norm_g[None, :], lp["w_row"], lp["w_t"], lp["w_g"])
    ya_t = _att_call(small, p_t, b, s, ropes[0], ropes[1], lp["qg_t"], lp["kg"], bdk, attn_sink)
    hm_fwd = _mlstm_call(False, mqk, p_t, g_t, b, s, lp["mdirp"])
    hm_t = _mlstm_call(True, mqk, p_t, g_t, b, s, lp["mdirp"], hm_fwd, lp["ng_t"])
    ys = _ssd_calls(xbc, small, b, s, lp["sdirp"], lp["conv_w8"], lp["conv_b"], lp["dskip"])
    out = _final_call(x2, ya_t, hm_t, ys.reshape(b * s, BRANCH), norm_g[None, :], ssm_norm_g[None, :],
                      lp["w2"], wa, wb, wc, wo)
    return out.reshape(b, s, D_MODEL)


def kernel(x_prompt, x_sample, norm_g, w_in, q_norm_g, k_norm_g, attn_sink, w_att_out, mlstm_i_b, mlstm_f_b,
           mlstm_norm_g, w_mlstm_out, conv_w, conv_b, a_log, dt_bias, d_skip, ssm_norm_g, w_ssm_out, w_out):
    depth = w_in.shape[0]
    bdk = _block_diag_mean(LANES)
    ropes = {x.shape[1]: _rope_tables(x.shape[1]) for x in (x_prompt, x_sample)}
    ys = [x_prompt, x_sample]
    for l in range(depth):
        lp = _layer_params(w_in[l], q_norm_g[l], k_norm_g[l], mlstm_i_b[l], mlstm_f_b[l], mlstm_norm_g[l],
                           a_log[l], dt_bias[l], conv_w[l], conv_b[l], d_skip[l])
        wa, wb, wc, wo = (w.astype(BF16) for w in (w_att_out[l], w_mlstm_out[l], w_ssm_out[l], w_out[l]))
        ys = [_layer(y, lp, ropes[y.shape[1]], bdk, norm_g[l], attn_sink[l], ssm_norm_g[l], wa, wb, wc, wo)
              for y in ys]
    return tuple(ys)
```

```python
import functools

import jax
import jax.numpy as jnp
import numpy as np
from jax import lax
from jax.experimental import pallas as pl
from jax.experimental.pallas import tpu as pltpu

F32 = jnp.float32
BF16 = jnp.bfloat16

D_MODEL = 1024
BRANCH = 512
RMS_EPS = 1e-6
CHUNK = 128
LANES = 128
SUBLANES = 8
BF16_ROWS = 16
ATT_HEADS = 8
ATT_HEAD_DIM = 64
ROT_HALF = 8
ROPE_THETA = 500000.0
MLSTM_HEADS = 4
MLSTM_DK = 128
MLSTM_BLOCK = 8
SSM_HEADS = 8
SSM_CONV = 5
HALO = BF16_ROWS
NEG = -1e30
LOG2E = float(np.log2(np.e))

MQK_W = 1024
XBC_W = 768
SMALL_W = 384
ROW_W = MQK_W + XBC_W + SMALL_W
T_MV, T_AQ, T_AV = 0, 512, 1024
T_ROWS = 1152
G_ROWS = 32
W2_WIDTH = 5120
PROJ_TM = 1024
FINAL_TM = 512

VMEM_LIMIT = 56 * 1024 * 1024


def _dot(a, b):
    return jnp.dot(a, b, preferred_element_type=F32)


def _dot_nt(a, b):
    return lax.dot_general(a, b, (((1,), (1,)), ((), ())), preferred_element_type=F32)


def _split3(x):
    h1 = x.astype(BF16)
    r1 = x - h1.astype(F32)
    h2 = r1.astype(BF16)
    r2 = r1 - h2.astype(F32)
    return h1, h2, r2.astype(BF16)


def _cum_left(mask_bf, x):
    h1, h2, h3 = _split3(x)
    return _dot(mask_bf, h1) + _dot(mask_bf, h2) + _dot(mask_bf, h3)


def _cum_right(x, mask_bf):
    h1, h2, h3 = _split3(x)
    return _dot(h1, mask_bf) + _dot(h2, mask_bf) + _dot(h3, mask_bf)


def _sigmoid(x):
    return 1.0 / (1.0 + jnp.exp(-x))


def _silu(x):
    return x * _sigmoid(x)


def _softplus(x):
    return jnp.maximum(x, 0.0) + jnp.log1p(jnp.exp(-jnp.abs(x)))


def _log_sigmoid(x):
    return -_softplus(-x)


def _rms(x, g_row):
    ms = jnp.mean(x * x, axis=-1, keepdims=True)
    return x * lax.rsqrt(ms + RMS_EPS) * g_row


def _params(n_axes):
    return pltpu.CompilerParams(dimension_semantics=("arbitrary",) * n_axes, vmem_limit_bytes=VMEM_LIMIT)


def _proj_kernel(x_ref, ng_ref, wr_ref, wt_ref, wg_ref, mqk_ref, xbc_ref, small_ref, t_ref, g_ref):
    h = _rms(x_ref[...], ng_ref[...]).astype(BF16)
    for j0 in range(0, MQK_W, 512):
        mqk_ref[:, j0:j0 + 512] = _dot(h, wr_ref[:, j0:j0 + 512]).astype(BF16)
    xbc_ref[...] = _dot(h, wr_ref[:, MQK_W:MQK_W + XBC_W]).astype(BF16)
    small_ref[...] = _dot(h, wr_ref[:, MQK_W + XBC_W:])
    n_chunks = PROJ_TM // CHUNK
    for r0 in range(0, T_ROWS, 384):
        res = _dot_nt(wt_ref[r0:r0 + 384, :], h).astype(BF16)
        for j in range(n_chunks):
            t_ref[j, r0:r0 + 384, :] = res[:, j * CHUNK:(j + 1) * CHUNK]
    gates = _dot_nt(wg_ref[...], h)
    for j in range(n_chunks):
        g_ref[j] = gates[:, j * CHUNK:(j + 1) * CHUNK]


def _proj_call(x2, ng, w_row, w_t, w_g):
    t = x2.shape[0]
    tm = PROJ_TM
    full = lambda shape: pl.BlockSpec(shape, lambda i: (0, 0), pipeline_mode=pl.Buffered(1))
    rows = lambda width: pl.BlockSpec((tm, width), lambda i: (i, 0))
    chunks = lambda r: pl.BlockSpec((tm // CHUNK, r, CHUNK), lambda i: (i, 0, 0))
    return pl.pallas_call(
        _proj_kernel,
        grid=(t // tm,),
        in_specs=[rows(D_MODEL), full((1, D_MODEL)), full((D_MODEL, ROW_W)), full((T_ROWS, D_MODEL)),
                  full((G_ROWS, D_MODEL))],
        out_specs=[rows(MQK_W), rows(XBC_W), rows(SMALL_W), chunks(T_ROWS), chunks(G_ROWS)],
        out_shape=[
            jax.ShapeDtypeStruct((t, MQK_W), BF16),
            jax.ShapeDtypeStruct((t, XBC_W), BF16),
            jax.ShapeDtypeStruct((t, SMALL_W), F32),
            jax.ShapeDtypeStruct((t // CHUNK, T_ROWS, CHUNK), BF16),
            jax.ShapeDtypeStruct((t // CHUNK, G_ROWS, CHUNK), F32),
        ],
        compiler_params=_params(1),
        name="proj",
    )(x2, ng, w_row, w_t, w_g)


def _rope_rows(x, tab):
    w = x.shape[1]
    return (x * tab[:, :w]
            + pltpu.roll(x, w - ROT_HALF, 1) * tab[:, w:2 * w]
            + pltpu.roll(x, ROT_HALF, 1) * tab[:, 2 * w:])


ATT_QB = 8


def _att_kernel(kp_ref, kc_ref, kn_ref, qt_ref, vp_ref, vc_ref, vn_ref, rp_ref, rc_ref, rn_ref,
                rt_ref, qg_ref, kg_ref, bdk_ref, sink_ref, o_ref):
    i = pl.program_id(1)
    ni = pl.num_programs(1)
    win = 3 * CHUNK
    nkeys = (ATT_QB + 2) * CHUNK

    k = jnp.concatenate([kp_ref[...], kc_ref[...], kn_ref[...]], axis=0)
    tab_k = jnp.concatenate([rp_ref[...], rc_ref[...], rn_ref[...]], axis=0)
    kms = _dot((k * k).astype(BF16), bdk_ref[...])
    kr = _rope_rows(k * lax.rsqrt(kms + RMS_EPS) * kg_ref[...], tab_k)
    lane = lax.broadcasted_iota(jnp.int32, (nkeys, LANES), 1)
    lo = lane < ATT_HEAD_DIM
    k_sw = pltpu.roll(kr, ATT_HEAD_DIM, 1)
    zero = jnp.zeros_like(kr)
    k_var = [[jnp.where(lo, kr, zero).astype(BF16), jnp.where(lo, zero, k_sw).astype(BF16)],
             [jnp.where(lo, k_sw, zero).astype(BF16), jnp.where(lo, zero, kr).astype(BF16)]]

    v_t = jnp.concatenate([vp_ref[...]] + [vc_ref[j] for j in range(ATT_QB)] + [vn_ref[...]], axis=1)

    s_idx = lax.broadcasted_iota(jnp.int32, (win, 2 * CHUNK), 0)
    t_idx = lax.broadcasted_iota(jnp.int32, (win, 2 * CHUNK), 1) % CHUNK
    band = (s_idx >= t_idx) & (s_idx <= t_idx + 2 * CHUNK)
    first_head = lax.broadcasted_iota(jnp.int32, (1, 2 * CHUNK), 1) < CHUNK

    s2 = []
    for qb in range(ATT_QB):
        cos = rt_ref[qb, 0:ROT_HALF, :]
        sin = rt_ref[qb, ROT_HALF:, :]
        q_heads = []
        for h in range(ATT_HEADS):
            x = qt_ref[qb, h * ATT_HEAD_DIM:(h + 1) * ATT_HEAD_DIM, :].astype(F32)
            ms = jnp.mean(x * x, axis=0, keepdims=True)
            xn = x * lax.rsqrt(ms + RMS_EPS) * qg_ref[...]
            x1 = xn[0:ROT_HALF]
            x2 = xn[ROT_HALF:2 * ROT_HALF]
            xr = jnp.concatenate([x1 * cos - x2 * sin, x2 * cos + x1 * sin, xn[2 * ROT_HALF:]], axis=0)
            q_heads.append((xr * (ATT_HEAD_DIM ** -0.5 * LOG2E)).astype(BF16))
        keys = slice(qb * CHUNK, qb * CHUNK + win)
        for g in range(2):
            lhs = jnp.concatenate([k_var[g][0][keys], k_var[g][1][keys]], axis=0)
            rhs = jnp.concatenate(
                [jnp.concatenate([q_heads[4 * g + 2 * j], q_heads[4 * g + 2 * j + 1]], axis=0)
                 for j in range(2)], axis=1)
            s2.append(_dot(lhs, rhs))
    ps, dens = [], []
    for qb in range(ATT_QB):
        valid = band
        if qb == 0:
            valid = valid & ((s_idx >= CHUNK) | (i > 0))
        if qb == ATT_QB - 1:
            valid = valid & ((s_idx < 2 * CHUNK) | (i < ni - 1))
        bias = jnp.where(valid, 0.0, NEG)
        for g in range(2):
            for half in range(2):
                s = s2[2 * qb + g][half * win:(half + 1) * win] + bias
                sink = jnp.where(first_head, sink_ref[4 * g + half], sink_ref[4 * g + 2 + half]) * LOG2E
                mx = jnp.maximum(jnp.max(s, axis=0, keepdims=True), sink)
                p = jnp.exp2(s - mx)
                dens.append(jnp.sum(p, axis=0, keepdims=True) + jnp.exp2(sink - mx))
                ps.append(p.astype(BF16))
    for qb in range(ATT_QB):
        outs = [None] * ATT_HEADS
        for g in range(2):
            vg = v_t[g * ATT_HEAD_DIM:(g + 1) * ATT_HEAD_DIM, qb * CHUNK:qb * CHUNK + win]
            for half in range(2):
                n = 4 * qb + 2 * g + half
                o = _dot(vg, ps[n]) * (1.0 / dens[n])
                outs[4 * g + half] = o[:, :CHUNK]
                outs[4 * g + 2 + half] = o[:, CHUNK:]
        o_ref[qb] = jnp.concatenate(outs, axis=0)


def _att_call(small, p_t, b, s, rope_rows, rope_t, qg_t, kg, bdk, sink):
    nb = s // CHUNK
    ni = nb // ATT_QB
    small3 = small.reshape(b, s, SMALL_W)
    prev = lambda i: jnp.maximum(ATT_QB * i - 1, 0)
    nxt = lambda i: jnp.minimum(ATT_QB * i + ATT_QB, nb - 1)
    av = T_AV // LANES
    return pl.pallas_call(
        _att_kernel,
        grid=(b, ni),
        in_specs=[
            pl.BlockSpec((None, CHUNK, LANES), lambda bi, i: (bi, prev(i), 0)),
            pl.BlockSpec((None, ATT_QB * CHUNK, LANES), lambda bi, i: (bi, i, 0)),
            pl.BlockSpec((None, CHUNK, LANES), lambda bi, i: (bi, nxt(i), 0)),
            pl.BlockSpec((ATT_QB, BRANCH, CHUNK), lambda bi, i: (bi * ni + i, T_AQ // BRANCH, 0)),
            pl.BlockSpec((None, LANES, CHUNK), lambda bi, i: (bi * nb + prev(i), av, 0)),
            pl.BlockSpec((ATT_QB, LANES, CHUNK), lambda bi, i: (bi * ni + i, av, 0)),
            pl.BlockSpec((None, LANES, CHUNK), lambda bi, i: (bi * nb + nxt(i), av, 0)),
            pl.BlockSpec((CHUNK, 3 * LANES), lambda bi, i: (prev(i), 0)),
            pl.BlockSpec((ATT_QB * CHUNK, 3 * LANES), lambda bi, i: (i, 0)),
            pl.BlockSpec((CHUNK, 3 * LANES), lambda bi, i: (nxt(i), 0)),
            pl.BlockSpec((ATT_QB, 2 * ROT_HALF, CHUNK), lambda bi, i: (i, 0, 0)),
            pl.BlockSpec((ATT_HEAD_DIM, LANES), lambda bi, i: (0, 0)),
            pl.BlockSpec((1, LANES), lambda bi, i: (0, 0)),
            pl.BlockSpec((LANES, LANES), lambda bi, i: (0, 0)),
            pl.BlockSpec(memory_space=pltpu.SMEM),
        ],
        out_specs=pl.BlockSpec((ATT_QB, BRANCH, CHUNK), lambda bi, i: (bi * ni + i, 0, 0)),
        out_shape=jax.ShapeDtypeStruct((b * nb, BRANCH, CHUNK), F32),
        compiler_params=_params(2),
        name="att",
    )(small3, small3, small3, p_t, p_t, p_t, p_t, rope_rows, rope_rows, rope_rows, rope_t,
      qg_t, kg, bdk, sink)


def _scan_masks(reverse):
    r = lax.broadcasted_iota(jnp.int32, (CHUNK, CHUNK), 0)
    c = lax.broadcasted_iota(jnp.int32, (CHUNK, CHUNK), 1)
    causal = (r <= c) if reverse else (r >= c)
    anti = (r >= c) if reverse else (r <= c)
    return causal, anti


def _scan_block(reverse, n):
    return (lambda c: n - 1 - c) if reverse else (lambda c: c)


AUG = MLSTM_DK + BF16_ROWS
LOG_QK_SCALE = float(np.log(MLSTM_DK ** -0.5))


def _mlstm_kernel(reverse, qk_ref, vt_ref, g_ref, dp_ref, *rest):
    if reverse:
        prev_ref, ng_ref, o_ref, st_sc, m_sc = rest
    else:
        o_ref, st_sc, m_sc = rest
    nblk = MLSTM_BLOCK
    rows = nblk * SUBLANES

    @pl.when(pl.program_id(1) == 0)
    def _():
        st_sc[...] = jnp.zeros_like(st_sc)
        m_sc[...] = jnp.zeros_like(m_sc)

    _, anti = _scan_masks(reverse)
    g = g_ref[...]
    ig = (g[:, 0:SUBLANES, :] + dp_ref[0:SUBLANES, :]).reshape(rows, CHUNK)
    lf = _log_sigmoid(g[:, SUBLANES:, :] + dp_ref[SUBLANES:, :]).reshape(rows, CHUNK)
    b = _cum_right(lf, anti.astype(BF16))
    a = ig - b
    a_max = jnp.max(a, axis=1, keepdims=True)
    b_tot = jnp.sum(lf, axis=1, keepdims=True)
    a_col = jnp.concatenate([a * LOG2E, jnp.zeros((CHUNK - rows, CHUNK), F32)], axis=0).T
    ones = jnp.ones((AUG - MLSTM_DK, CHUNK), BF16)
    scale = MLSTM_DK ** -0.5

    m_st = m_sc[...][:, 0:1]
    st = [st_sc[h] for h in range(MLSTM_HEADS)]
    order = range(nblk - 1, -1, -1) if reverse else range(nblk)
    for j in order:
        sl = slice(j * SUBLANES, (j + 1) * SUBLANES)
        tok = slice(j * CHUNK, (j + 1) * CHUNK)
        mm = jnp.maximum(m_st, a_max[sl])
        wk = jnp.exp(a[sl] - mm) * scale
        decay = jnp.exp(m_st - mm)
        ks, vts, sts = [], [], []
        for h in range(MLSTM_HEADS):
            q = qk_ref[tok, h * MLSTM_DK:(h + 1) * MLSTM_DK]
            k = qk_ref[tok, BRANCH + h * MLSTM_DK:BRANCH + (h + 1) * MLSTM_DK]
            ks.append(k)
            vts.append(jnp.concatenate([vt_ref[j, h * MLSTM_DK:(h + 1) * MLSTM_DK, :], ones], axis=0))
            sts.append(_dot_nt(jnp.concatenate([k, st[h].astype(BF16)], axis=0), q))
        scs, w_inters, m_ts = [], [], []
        for h in range(MLSTM_HEADS):
            col = j * SUBLANES + h
            a_m = jnp.where(anti, jnp.broadcast_to(a_col[:, col:col + 1], (CHUNK, CHUNK)), NEG)
            m_h = m_st[h:h + 1, :]
            u2 = jnp.maximum(m_h * LOG2E, jnp.max(a_m, axis=0, keepdims=True))
            u = u2 * (1.0 / LOG2E)
            m_ts.append(b[col:col + 1, :] + u)
            w_inters.append(jnp.exp(m_h - u))
            scs.append((sts[h][:CHUNK] * jnp.exp2(a_m - (u2 - LOG_QK_SCALE * LOG2E))).astype(BF16))
        outs = []
        for h in range(MLSTM_HEADS):
            nd = _dot(vts[h], scs[h]) + w_inters[h] * sts[h][CHUNK:]
            den = jnp.maximum(jnp.abs(nd[MLSTM_DK:MLSTM_DK + 1, :]), jnp.exp(-m_ts[h]))
            outs.append(nd[:MLSTM_DK] * (1.0 / den))
        for h in range(MLSTM_HEADS):
            vw = (vts[h].astype(F32) * wk[h:h + 1, :]).astype(BF16)
            st[h] = decay[h:h + 1, :] * st[h] + _dot(vw, ks[h])
        m_st = b_tot[sl] + mm
        if reverse:
            normed = []
            for h in range(MLSTM_HEADS):
                x = prev_ref[j, h * MLSTM_DK:(h + 1) * MLSTM_DK, :] + outs[h]
                ms = jnp.mean(x * x, axis=0, keepdims=True)
                normed.append(x * lax.rsqrt(ms + RMS_EPS))
            o_ref[j] = jnp.concatenate(normed, axis=0) * ng_ref[...]
        else:
            o_ref[j] = jnp.concatenate(outs, axis=0)
    for h in range(MLSTM_HEADS):
        st_sc[h] = st[h]
    m_sc[...] = jnp.broadcast_to(m_st, (SUBLANES, LANES))


def _mlstm_call(reverse, mqk, p_t, g_t, b, s, dirp, prev=None, ng_t=None):
    nb = s // (CHUNK * MLSTM_BLOCK)
    blk = _scan_block(reverse, nb)
    mqk3 = mqk.reshape(b, s, MQK_W)
    d = int(reverse)
    out_spec = pl.BlockSpec((MLSTM_BLOCK, BRANCH, CHUNK), lambda bi, c: (bi * nb + blk(c), 0, 0))
    in_specs = [
        pl.BlockSpec((None, CHUNK * MLSTM_BLOCK, MQK_W), lambda bi, c: (bi, blk(c), 0)),
        pl.BlockSpec((MLSTM_BLOCK, BRANCH, CHUNK), lambda bi, c: (bi * nb + blk(c), T_MV // BRANCH, 0)),
        pl.BlockSpec((MLSTM_BLOCK, 2 * SUBLANES, CHUNK), lambda bi, c: (bi * nb + blk(c), d, 0)),
        pl.BlockSpec((None, 2 * SUBLANES, LANES), lambda bi, c: (d, 0, 0)),
    ]
    args = [mqk3, p_t, g_t, dirp]
    if reverse:
        in_specs += [out_spec, pl.BlockSpec((BRANCH, LANES), lambda bi, c: (0, 0))]
        args += [prev, ng_t]
    return pl.pallas_call(
        functools.partial(_mlstm_kernel, reverse),
        grid=(b, nb),
        in_specs=in_specs,
        out_specs=out_spec,
        out_shape=jax.ShapeDtypeStruct((b * s // CHUNK, BRANCH, CHUNK), F32),
        scratch_shapes=[
            pltpu.VMEM((MLSTM_HEADS, AUG, MLSTM_DK), F32),
            pltpu.VMEM((SUBLANES, LANES), F32),
        ],
        compiler_params=_params(2),
        name="mlstm_bwd" if reverse else "mlstm_fwd",
    )(*args)


SSD_BLOCK = 4


def _ssd_local(reverse, xs_b, bt, cm_b, g, dp_ref):
    causal, anti = _scan_masks(reverse)
    dt = _softplus(g + dp_ref[0:1, :])
    da = dt * (-jnp.exp(dp_ref[1:2, :]) * LOG2E)
    da_t = da.T
    acs_c = _cum_left(causal.astype(BF16), da)
    acs_r = _cum_right(da_t, anti.astype(BF16)) - jnp.log2(dt.T)
    tot = jnp.sum(da_t, axis=1, keepdims=True)

    lane = lax.broadcasted_iota(jnp.int32, (CHUNK, LANES), 1)
    lo = lane < 64
    lo_row = lo[0:1, :]
    bt_b = bt.astype(BF16)
    zero_b = jnp.zeros((CHUNK, LANES), BF16)
    cb = [_dot(jnp.where(lo, cm_b, zero_b), bt_b), _dot(jnp.where(lo, zero_b, cm_b), bt_b)]

    out = []
    for j in range(4):
        grp = j // 2
        m_parts, bw_parts, eac, edec = [], [], [], []
        for e in (2 * j, 2 * j + 1):
            ln = SSM_HEADS + e
            ac = acs_c[:, ln:ln + 1]
            ar = acs_r[ln:ln + 1, :]
            te = tot[ln:ln + 1, :]
            m_parts.append((cb[grp] * jnp.exp2(jnp.where(causal, ac - ar, NEG))).astype(BF16))
            bw_parts.append((bt * jnp.exp2(te - ar)).astype(BF16))
            eac.append(jnp.exp2(ac))
            edec.append(jnp.exp2(te))
        lhs = jnp.concatenate([jnp.concatenate(m_parts, axis=1),
                               jnp.concatenate(bw_parts, axis=1)], axis=0)
        xp = xs_b[:, j * LANES:(j + 1) * LANES]
        x2 = jnp.concatenate([jnp.where(lo, xp, zero_b), jnp.where(lo, zero_b, xp)], axis=0)
        out.append((_dot(lhs, x2), jnp.where(lo, eac[0], eac[1]), jnp.where(lo_row, edec[0], edec[1])))
    return out


def _ssd_carry(local, cm_b, st):
    row = lax.broadcasted_iota(jnp.int32, (CHUNK, LANES), 0)
    ys = []
    for j in range(4):
        res, eac, edec = local[j]
        grp = j // 2
        in_grp = (row >= 64 * grp) & (row < 64 * grp + 64)
        ys.append(res[:CHUNK] + _dot(cm_b, st[j].astype(BF16)) * eac)
        st[j] = st[j] * edec + jnp.where(in_grp, res[CHUNK:], 0.0)
    return jnp.concatenate(ys, axis=1)


def _ssd_fwd_kernel(x_ref, xp_ref, xn_ref, g_ref, dp_ref, cw_ref, cb_ref, dsk_ref, y_ref, xc_ref,
                    st_sc, pad_sc):
    c = pl.program_id(1)
    nc = pl.num_programs(1)
    rows = SSD_BLOCK * CHUNK

    @pl.when(c == 0)
    def _():
        st_sc[...] = jnp.zeros_like(st_sc)

    pad_sc[0:HALO, :] = jnp.where(c > 0, xp_ref[...].astype(F32), 0.0)
    pad_sc[HALO:HALO + rows, :] = x_ref[...].astype(F32)
    pad_sc[HALO + rows:, :] = jnp.where(c < nc - 1, xn_ref[...].astype(F32), 0.0)
    pad = pad_sc[...]
    n_pad = rows + 2 * HALO
    mid = SSM_CONV // 2
    shifted = [pad if t == mid else pltpu.roll(pad, (n_pad - (t - mid)) % n_pad, 0) for t in range(SSM_CONV)]
    xcs, local = [], []
    for j in range(SSD_BLOCK):
        tok = slice(j * CHUNK, (j + 1) * CHUNK)
        acc = jnp.broadcast_to(cb_ref[...], (CHUNK, XBC_W))
        for t in range(SSM_CONV):
            acc = acc + shifted[t][HALO + j * CHUNK:HALO + (j + 1) * CHUNK, :] * cw_ref[t:t + 1, :]
        xc = _silu(acc)
        xc_b = xc.astype(BF16)
        xc_ref[tok, :] = xc_b
        xcs.append((xc, xc_b))
        local.append(_ssd_local(False, xc_b[:, :BRANCH], xc[:, BRANCH:BRANCH + LANES].T,
                                xc_b[:, BRANCH + LANES:], g_ref[tok, :], dp_ref))
    st = [st_sc[j] for j in range(4)]
    for j in range(SSD_BLOCK):
        xc, xc_b = xcs[j]
        y = _ssd_carry(local[j], xc_b[:, BRANCH + LANES:], st)
        y_ref[j * CHUNK:(j + 1) * CHUNK, :] = y + dsk_ref[...] * xc[:, :BRANCH]
    for j in range(4):
        st_sc[j] = st[j]


def _ssd_bwd_kernel(xc_ref, g_ref, dp_ref, prev_ref, y_ref, st_sc):
    @pl.when(pl.program_id(1) == 0)
    def _():
        st_sc[...] = jnp.zeros_like(st_sc)

    order = range(SSD_BLOCK - 1, -1, -1)
    local = {}
    for j in order:
        tok = slice(j * CHUNK, (j + 1) * CHUNK)
        bt = xc_ref[tok, BRANCH:BRANCH + LANES].astype(F32).T
        local[j] = _ssd_local(True, xc_ref[tok, :BRANCH], bt, xc_ref[tok, BRANCH + LANES:],
                              g_ref[tok, :], dp_ref)
    st = [st_sc[j] for j in range(4)]
    for j in order:
        tok = slice(j * CHUNK, (j + 1) * CHUNK)
        y_ref[tok, :] = prev_ref[tok, :] + _ssd_carry(local[j], xc_ref[tok, BRANCH + LANES:], st)
    for j in range(4):
        st_sc[j] = st[j]


def _ssd_calls(xbc, small, b, s, dirp, conv_w8, conv_b, dskip):
    rows = SSD_BLOCK * CHUNK
    nb = s // rows
    per = rows // HALO
    last = s // HALO - 1
    xbc3 = xbc.reshape(b, s, XBC_W)
    small3 = small.reshape(b, s, SMALL_W)
    state = pltpu.VMEM((4, CHUNK, LANES), F32)
    y_shape = jax.ShapeDtypeStruct((b, s, BRANCH), F32)
    const = lambda shape: pl.BlockSpec(shape, lambda bi, c: (0,) * len(shape))

    def block_spec(width, blk, col=0):
        return pl.BlockSpec((None, rows, width), lambda bi, c: (bi, blk(c), col))

    fwd = _scan_block(False, nb)
    y_fwd, xc = pl.pallas_call(
        _ssd_fwd_kernel,
        grid=(b, nb),
        in_specs=[
            block_spec(XBC_W, fwd),
            pl.BlockSpec((None, HALO, XBC_W), lambda bi, c: (bi, jnp.maximum(c * per - 1, 0), 0)),
            pl.BlockSpec((None, HALO, XBC_W), lambda bi, c: (bi, jnp.minimum(c * per + per, last), 0)),
            block_spec(LANES, fwd, 1),
            pl.BlockSpec((None, SUBLANES, LANES), lambda bi, c: (0, 0, 0)),
            const((SUBLANES, XBC_W)), const((1, XBC_W)), const((1, BRANCH)),
        ],
        out_specs=[block_spec(BRANCH, fwd), block_spec(XBC_W, fwd)],
        out_shape=[y_shape, jax.ShapeDtypeStruct((b, s, XBC_W), BF16)],
        scratch_shapes=[state, pltpu.VMEM((rows + 2 * HALO, XBC_W), F32)],
        compiler_params=_params(2),
        name="ssd_fwd",
    )(xbc3, xbc3, xbc3, small3, dirp, conv_w8, conv_b, dskip)

    bwd = _scan_block(True, nb)
    return pl.pallas_call(
        _ssd_bwd_kernel,
        grid=(b, nb),
        in_specs=[
            block_spec(XBC_W, bwd),
            block_spec(LANES, bwd, 2),
            pl.BlockSpec((None, SUBLANES, LANES), lambda bi, c: (1, 0, 0)),
            block_spec(BRANCH, bwd),
        ],
        out_specs=block_spec(BRANCH, bwd),
        out_shape=y_shape,
        scratch_shapes=[state],
        compiler_params=_params(2),
        name="ssd_bwd",
    )(xc, small3, dirp, y_fwd)


def _final_kernel(x_ref, yat_ref, hmt_ref, ys_ref, ng_ref, sg_ref, w2_ref, wa_ref, wb_ref, wc_ref,
                  wo_ref, o_ref):
    x = x_ref[...]
    h = _rms(x, ng_ref[...]).astype(BF16)

    def proj(k):
        return _dot(h, w2_ref[:, k * BRANCH:(k + 1) * BRANCH])

    def gate(k):
        lo = 4 * BRANCH + k * D_MODEL
        return _sigmoid(_dot(h, w2_ref[:, lo:lo + D_MODEL]))

    def token_major(ref):
        return jnp.concatenate([ref[j].T for j in range(FINAL_TM // CHUNK)], axis=0)

    ya = token_major(yat_ref) * _silu(proj(0))
    merged = gate(0) * _dot(ya.astype(BF16), wa_ref[...])
    yb = token_major(hmt_ref) * _sigmoid(proj(1)) * _silu(proj(2))
    merged = merged + gate(1) * _dot(yb.astype(BF16), wb_ref[...])
    yc = _rms(ys_ref[...] * _silu(proj(3)), sg_ref[...])
    merged = merged + gate(2) * _dot(yc.astype(BF16), wc_ref[...])
    o_ref[...] = x + _dot(merged.astype(BF16), wo_ref[...])


def _final_call(x2, ya_t, hm_t, ys, ng, sg, w2, wa, wb, wc, wo):
    t = x2.shape[0]
    tm = FINAL_TM
    row = lambda width: pl.BlockSpec((tm, width), lambda i: (i, 0))
    chunked = pl.BlockSpec((tm // CHUNK, BRANCH, CHUNK), lambda i: (i, 0, 0))
    full = lambda shape: pl.BlockSpec(shape, lambda i: (0, 0), pipeline_mode=pl.Buffered(1))
    return pl.pallas_call(
        _final_kernel,
        grid=(t // tm,),
        in_specs=[
            row(D_MODEL), chunked, chunked, row(BRANCH),
            full((1, D_MODEL)), full((1, BRANCH)),
            full((D_MODEL, W2_WIDTH)), full((BRANCH, D_MODEL)), full((BRANCH, D_MODEL)),
            full((BRANCH, D_MODEL)), full((D_MODEL, D_MODEL)),
        ],
        out_specs=row(D_MODEL),
        out_shape=jax.ShapeDtypeStruct((t, D_MODEL), F32),
        compiler_params=_params(1),
        name="final",
    )(x2, ya_t, hm_t, ys, ng, sg, w2, wa, wb, wc, wo)


_IN_SPLITS = (512, 128, 128, 512, 512, 512, 512, 512, 8, 8, 512, 768, 16, 512, 3072)
_NAMES = ("aq", "ak", "av", "az", "mq", "mk", "mv", "mo", "mi", "mf", "mz", "sxbc", "sdt", "sz", "gates")


def _split_w_in(w):
    bounds = np.cumsum((0,) + _IN_SPLITS)
    return {n: w[:, bounds[i]:bounds[i + 1]] for i, n in enumerate(_NAMES)}


def _layer_params(w_in, q_norm_g, k_norm_g, mlstm_i_b, mlstm_f_b, mlstm_norm_g, a_log, dt_bias, conv_w,
                  conv_b, d_skip):
    p = _split_w_in(w_in)
    zc = lambda n: jnp.zeros((D_MODEL, n), F32)
    ssd_gates = []
    for d in range(2):
        ssd_gates += [zc(SSM_HEADS), p["sdt"][:, 8 * d:8 * d + 8], zc(LANES - 2 * SSM_HEADS)]
    w_row = jnp.concatenate([p["mq"], p["mk"], p["sxbc"], p["ak"]] + ssd_gates, axis=1).astype(BF16)
    w_t = jnp.concatenate([p["mv"], p["aq"], p["av"]], axis=1).T.astype(BF16)
    m_gates = []
    for d in range(2):
        m_gates += [p["mi"][:, 4 * d:4 * d + 4], zc(4), p["mf"][:, 4 * d:4 * d + 4], zc(4)]
    w_g = jnp.concatenate(m_gates, axis=1).T.astype(BF16)
    w2 = jnp.concatenate([p["az"], p["mo"], p["mz"], p["sz"], p["gates"]], axis=1).astype(BF16)
    sdirp = jnp.zeros((2, SUBLANES, LANES), F32)
    sdirp = sdirp.at[:, 0, 8:16].set(dt_bias).at[:, 1, 8:16].set(a_log)
    mdirp = jnp.zeros((2, 2 * SUBLANES, LANES), F32)
    mdirp = mdirp.at[:, 0:4, :].set(mlstm_i_b[:, :, None]).at[:, 8:12, :].set(mlstm_f_b[:, :, None])
    return dict(
        w_row=w_row, w_t=w_t, w_g=w_g, w2=w2, sdirp=sdirp, mdirp=mdirp,
        qg_t=jnp.broadcast_to(q_norm_g[:, None], (ATT_HEAD_DIM, LANES)),
        kg=jnp.tile(k_norm_g, 2)[None, :],
        ng_t=jnp.broadcast_to(mlstm_norm_g[:, None], (BRANCH, LANES)),
        conv_w8=jnp.zeros((SUBLANES, XBC_W), F32).at[:SSM_CONV].set(conv_w), conv_b=conv_b[None, :],
        dskip=jnp.repeat(d_skip, 64)[None, :])


def _rope_tables(s_len):
    inv_freq = ROPE_THETA ** (-jnp.arange(ROT_HALF, dtype=F32) * 2.0 / (2 * ROT_HALF))
    ang = jnp.arange(s_len).astype(F32)[:, None] * inv_freq
    cos, sin = jnp.cos(ang), jnp.sin(ang)
    rest = ATT_HEAD_DIM - 2 * ROT_HALF
    one = jnp.ones((s_len, rest), F32)
    zer = jnp.zeros((s_len, rest), F32)
    z8 = jnp.zeros((s_len, ROT_HALF), F32)
    tc = jnp.concatenate([cos, cos, one], axis=1)
    t1 = jnp.concatenate([-sin, z8, zer], axis=1)
    t2 = jnp.concatenate([z8, sin, zer], axis=1)
    rows = jnp.concatenate([jnp.tile(t, (1, 2)) for t in (tc, t1, t2)], axis=1)
    nb = s_len // CHUNK
    t_tab = jnp.concatenate([cos.T.reshape(ROT_HALF, nb, CHUNK), sin.T.reshape(ROT_HALF, nb, CHUNK)], axis=0)
    return rows, t_tab.transpose(1, 0, 2)


def _block_diag_mean(width):
    blk = np.kron(np.eye(width // ATT_HEAD_DIM), np.full((ATT_HEAD_DIM, ATT_HEAD_DIM), 1.0 / ATT_HEAD_DIM))
    return jnp.asarray(blk, BF16)


def _layer(x, lp, ropes, bdk, norm_g, attn_sink, ssm_norm_g, wa, wb, wc, wo):
    b, s, _ = x.shape
    assert s % (CHUNK * max(MLSTM_BLOCK, SSD_BLOCK, ATT_QB)) == 0 and (b * s) % max(PROJ_TM, FINAL_TM) == 0, (b, s)
    x2 = x.reshape(b * s, D_MODEL)
    mqk, xbc, small, p_t, g_t = _proj_call(x2, norm_g[None, :], lp["w_row"], lp["w_t"], lp["w_g"])
    ya_t = _att_call(small, p_t, b, s, ropes[0], ropes[1], lp["qg_t"], lp["kg"], bdk, attn_sink)
    hm_fwd = _mlstm_call(False, mqk, p_t, g_t, b, s, lp["mdirp"])
    hm_t = _mlstm_call(True, mqk, p_t, g_t, b, s, lp["mdirp"], hm_fwd, lp["ng_t"])
    ys = _ssd_calls(xbc, small, b, s, lp["sdirp"], lp["conv_w8"], lp["conv_b"], lp["dskip"])
    out = _final_call(x2, ya_t, hm_t, ys.reshape(b * s, BRANCH), norm_g[None, :], ssm_norm_g[None, :],
                      lp["w2"], wa, wb, wc, wo)
    return out.reshape(b, s, D_MODEL)


def kernel(x_prompt, x_sample, norm_g, w_in, q_norm_g, k_norm_g, attn_sink, w_att_out, mlstm_i_b, mlstm_f_b,
           mlstm_norm_g, w_mlstm_out, conv_w, conv_b, a_log, dt_bias, d_skip, ssm_norm_g, w_ssm_out, w_out):
    depth = w_in.shape[0]
    bdk = _block_diag_mean(LANES)
    ropes = {x.shape[1]: _rope_tables(x.shape[1]) for x in (x_prompt, x_sample)}
    ys = [x_prompt, x_sample]
    for l in range(depth):
        lp = _layer_params(w_in[l], q_norm_g[l], k_norm_g[l], mlstm_i_b[l], mlstm_f_b[l], mlstm_norm_g[l],
                           a_log[l], dt_bias[l], conv_w[l], conv_b[l], d_skip[l])
        wa, wb, wc, wo = (w.astype(BF16) for w in (w_att_out[l], w_mlstm_out[l], w_ssm_out[l], w_out[l]))
        ys = [_layer(y, lp, ropes[y.shape[1]], bdk, norm_g[l], attn_sink[l], ssm_norm_g[l], wa, wb, wc, wo)
              for y in ys]
    return tuple(ys)
```

```python
import functools

import jax
import jax.numpy as jnp
import numpy as np
from jax import lax
from jax.experimental import pallas as pl
from jax.experimental.pallas import tpu as pltpu

F32 = jnp.float32
BF16 = jnp.bfloat16

D_MODEL = 1024
BRANCH = 512
RMS_EPS = 1e-6
CHUNK = 128
LANES = 128
SUBLANES = 8
BF16_ROWS = 16
ATT_HEADS = 8
ATT_HEAD_DIM = 64
ROT_HALF = 8
ROPE_THETA = 500000.0
MLSTM_HEADS = 4
MLSTM_DK = 128
MLSTM_BLOCK = 8
SSM_HEADS = 8
SSM_CONV = 5
HALO = BF16_ROWS
NEG = -1e30
LOG2E = float(np.log2(np.e))

MQK_W = 1024
XBC_W = 768
SMALL_W = 128
ROW_W = MQK_W + XBC_W + SMALL_W
T_MV, T_AQ, T_AV = 0, 512, 1024
T_ROWS = 1152
G_ROWS = 48
G_SSD = 32
W2_WIDTH = 5120
PROJ_TM = 1024
FINAL_TM = 512

VMEM_LIMIT = 56 * 1024 * 1024


def _dot(a, b):
    return jnp.dot(a, b, preferred_element_type=F32)


def _dot_nt(a, b):
    return lax.dot_general(a, b, (((1,), (1,)), ((), ())), preferred_element_type=F32)


def _split3(x):
    h1 = x.astype(BF16)
    r1 = x - h1.astype(F32)
    h2 = r1.astype(BF16)
    r2 = r1 - h2.astype(F32)
    return h1, h2, r2.astype(BF16)


def _cum_left(mask_bf, x):
    h1, h2, h3 = _split3(x)
    return _dot(mask_bf, h1) + _dot(mask_bf, h2) + _dot(mask_bf, h3)


def _cum_right(x, mask_bf):
    h1, h2, h3 = _split3(x)
    return _dot(h1, mask_bf) + _dot(h2, mask_bf) + _dot(h3, mask_bf)


def _sigmoid(x):
    return 1.0 / (1.0 + jnp.exp(-x))


def _silu(x):
    return x * _sigmoid(x)


def _softplus(x):
    return jnp.maximum(x, 0.0) + jnp.log1p(jnp.exp(-jnp.abs(x)))


def _log_sigmoid(x):
    return -_softplus(-x)


def _rms(x, g_row):
    ms = jnp.mean(x * x, axis=-1, keepdims=True)
    return x * lax.rsqrt(ms + RMS_EPS) * g_row


def _params(n_axes):
    return pltpu.CompilerParams(dimension_semantics=("arbitrary",) * n_axes, vmem_limit_bytes=VMEM_LIMIT)


def _proj_kernel(x_ref, ng_ref, wr_ref, wt_ref, wg_ref, mqk_ref, xbc_ref, small_ref, t_ref, g_ref):
    h = _rms(x_ref[...], ng_ref[...]).astype(BF16)
    for j0 in range(0, MQK_W, 512):
        mqk_ref[:, j0:j0 + 512] = _dot(h, wr_ref[:, j0:j0 + 512]).astype(BF16)
    xbc_ref[...] = _dot(h, wr_ref[:, MQK_W:MQK_W + XBC_W]).astype(BF16)
    small_ref[...] = _dot(h, wr_ref[:, MQK_W + XBC_W:])
    n_chunks = PROJ_TM // CHUNK
    for r0 in range(0, T_ROWS, 384):
        res = _dot_nt(wt_ref[r0:r0 + 384, :], h).astype(BF16)
        for j in range(n_chunks):
            t_ref[j, r0:r0 + 384, :] = res[:, j * CHUNK:(j + 1) * CHUNK]
    gates = _dot_nt(wg_ref[...], h)
    for j in range(n_chunks):
        g_ref[j] = gates[:, j * CHUNK:(j + 1) * CHUNK]


def _proj_call(x2, ng, w_row, w_t, w_g):
    t = x2.shape[0]
    tm = PROJ_TM
    full = lambda shape: pl.BlockSpec(shape, lambda i: (0, 0), pipeline_mode=pl.Buffered(1))
    rows = lambda width: pl.BlockSpec((tm, width), lambda i: (i, 0))
    chunks = lambda r: pl.BlockSpec((tm // CHUNK, r, CHUNK), lambda i: (i, 0, 0))
    return pl.pallas_call(
        _proj_kernel,
        grid=(t // tm,),
        in_specs=[rows(D_MODEL), full((1, D_MODEL)), full((D_MODEL, ROW_W)), full((T_ROWS, D_MODEL)),
                  full((G_ROWS, D_MODEL))],
        out_specs=[rows(MQK_W), rows(XBC_W), rows(SMALL_W), chunks(T_ROWS), chunks(G_ROWS)],
        out_shape=[
            jax.ShapeDtypeStruct((t, MQK_W), BF16),
            jax.ShapeDtypeStruct((t, XBC_W), BF16),
            jax.ShapeDtypeStruct((t, SMALL_W), F32),
            jax.ShapeDtypeStruct((t // CHUNK, T_ROWS, CHUNK), BF16),
            jax.ShapeDtypeStruct((t // CHUNK, G_ROWS, CHUNK), F32),
        ],
        compiler_params=_params(1),
        name="proj",
    )(x2, ng, w_row, w_t, w_g)


def _rope_rows(x, tab):
    w = x.shape[1]
    return (x * tab[:, :w]
            + pltpu.roll(x, w - ROT_HALF, 1) * tab[:, w:2 * w]
            + pltpu.roll(x, ROT_HALF, 1) * tab[:, 2 * w:])


ATT_QB = 8


def _att_kernel(kp_ref, kc_ref, kn_ref, qt_ref, vp_ref, vc_ref, vn_ref, rp_ref, rc_ref, rn_ref,
                rt_ref, qg_ref, kg_ref, bdk_ref, sink_ref, o_ref):
    i = pl.program_id(1)
    ni = pl.num_programs(1)
    win = 3 * CHUNK
    nkeys = (ATT_QB + 2) * CHUNK

    k = jnp.concatenate([kp_ref[...], kc_ref[...], kn_ref[...]], axis=0)
    tab_k = jnp.concatenate([rp_ref[...], rc_ref[...], rn_ref[...]], axis=0)
    kms = _dot((k * k).astype(BF16), bdk_ref[...])
    kr = _rope_rows(k * lax.rsqrt(kms + RMS_EPS) * kg_ref[...], tab_k)
    lane = lax.broadcasted_iota(jnp.int32, (nkeys, LANES), 1)
    lo = lane < ATT_HEAD_DIM
    k_sw = pltpu.roll(kr, ATT_HEAD_DIM, 1)
    zero = jnp.zeros_like(kr)
    k_var = [[jnp.where(lo, kr, zero).astype(BF16), jnp.where(lo, zero, k_sw).astype(BF16)],
             [jnp.where(lo, k_sw, zero).astype(BF16), jnp.where(lo, zero, kr).astype(BF16)]]

    v_t = jnp.concatenate([vp_ref[...]] + [vc_ref[j] for j in range(ATT_QB)] + [vn_ref[...]], axis=1)
    ones = jnp.ones((BF16_ROWS, win), BF16)

    s_idx = lax.broadcasted_iota(jnp.int32, (win, 2 * CHUNK), 0)
    t_idx = lax.broadcasted_iota(jnp.int32, (win, 2 * CHUNK), 1) % CHUNK
    band = (s_idx >= t_idx) & (s_idx <= t_idx + 2 * CHUNK)
    first_head = lax.broadcasted_iota(jnp.int32, (1, 2 * CHUNK), 1) < CHUNK

    s2 = []
    for qb in range(ATT_QB):
        cos = rt_ref[qb, 0:ROT_HALF, :]
        sin = rt_ref[qb, ROT_HALF:, :]
        q_heads = []
        for h in range(ATT_HEADS):
            x = qt_ref[qb, h * ATT_HEAD_DIM:(h + 1) * ATT_HEAD_DIM, :].astype(F32)
            ms = jnp.mean(x * x, axis=0, keepdims=True)
            xn = x * lax.rsqrt(ms + RMS_EPS) * qg_ref[...]
            x1 = xn[0:ROT_HALF]
            x2 = xn[ROT_HALF:2 * ROT_HALF]
            xr = jnp.concatenate([x1 * cos - x2 * sin, x2 * cos + x1 * sin, xn[2 * ROT_HALF:]], axis=0)
            q_heads.append((xr * (ATT_HEAD_DIM ** -0.5 * LOG2E)).astype(BF16))
        keys = slice(qb * CHUNK, qb * CHUNK + win)
        for g in range(2):
            lhs = jnp.concatenate([k_var[g][0][keys], k_var[g][1][keys]], axis=0)
            rhs = jnp.concatenate(
                [jnp.concatenate([q_heads[4 * g + 2 * j], q_heads[4 * g + 2 * j + 1]], axis=0)
                 for j in range(2)], axis=1)
            s2.append(_dot(lhs, rhs))
    ps, dens = [], []
    for qb in range(ATT_QB):
        valid = band
        if qb == 0:
            valid = valid & ((s_idx >= CHUNK) | (i > 0))
        if qb == ATT_QB - 1:
            valid = valid & ((s_idx < 2 * CHUNK) | (i < ni - 1))
        bias = jnp.where(valid, 0.0, NEG)
        for g in range(2):
            for half in range(2):
                s = s2[2 * qb + g][half * win:(half + 1) * win] + bias
                sink = jnp.where(first_head, sink_ref[4 * g + half], sink_ref[4 * g + 2 + half]) * LOG2E
                mx = jnp.maximum(jnp.max(s, axis=0, keepdims=True), sink)
                p = jnp.exp2(s - mx)
                dens.append(jnp.exp2(sink - mx))
                ps.append(p.astype(BF16))
    for qb in range(ATT_QB):
        outs = [None] * ATT_HEADS
        for g in range(2):
            vg = jnp.concatenate([v_t[g * ATT_HEAD_DIM:(g + 1) * ATT_HEAD_DIM, qb * CHUNK:qb * CHUNK + win], ones], axis=0)
            for half in range(2):
                n = 4 * qb + 2 * g + half
                od = _dot(vg, ps[n])
                o = od[:ATT_HEAD_DIM] * (1.0 / (od[ATT_HEAD_DIM:ATT_HEAD_DIM + 1] + dens[n]))
                outs[4 * g + half] = o[:, :CHUNK]
                outs[4 * g + 2 + half] = o[:, CHUNK:]
        o_ref[qb] = jnp.concatenate(outs, axis=0)


def _att_call(small, p_t, b, s, rope_rows, rope_t, qg_t, kg, bdk, sink):
    nb = s // CHUNK
    ni = nb // ATT_QB
    small3 = small.reshape(b, s, SMALL_W)
    prev = lambda i: jnp.maximum(ATT_QB * i - 1, 0)
    nxt = lambda i: jnp.minimum(ATT_QB * i + ATT_QB, nb - 1)
    av = T_AV // LANES
    return pl.pallas_call(
        _att_kernel,
        grid=(b, ni),
        in_specs=[
            pl.BlockSpec((None, CHUNK, LANES), lambda bi, i: (bi, prev(i), 0)),
            pl.BlockSpec((None, ATT_QB * CHUNK, LANES), lambda bi, i: (bi, i, 0)),
            pl.BlockSpec((None, CHUNK, LANES), lambda bi, i: (bi, nxt(i), 0)),
            pl.BlockSpec((ATT_QB, BRANCH, CHUNK), lambda bi, i: (bi * ni + i, T_AQ // BRANCH, 0)),
            pl.BlockSpec((None, LANES, CHUNK), lambda bi, i: (bi * nb + prev(i), av, 0)),
            pl.BlockSpec((ATT_QB, LANES, CHUNK), lambda bi, i: (bi * ni + i, av, 0)),
            pl.BlockSpec((None, LANES, CHUNK), lambda bi, i: (bi * nb + nxt(i), av, 0)),
            pl.BlockSpec((CHUNK, 3 * LANES), lambda bi, i: (prev(i), 0)),
            pl.BlockSpec((ATT_QB * CHUNK, 3 * LANES), lambda bi, i: (i, 0)),
            pl.BlockSpec((CHUNK, 3 * LANES), lambda bi, i: (nxt(i), 0)),
            pl.BlockSpec((ATT_QB, 2 * ROT_HALF, CHUNK), lambda bi, i: (i, 0, 0)),
            pl.BlockSpec((ATT_HEAD_DIM, LANES), lambda bi, i: (0, 0)),
            pl.BlockSpec((1, LANES), lambda bi, i: (0, 0)),
            pl.BlockSpec((LANES, LANES), lambda bi, i: (0, 0)),
            pl.BlockSpec(memory_space=pltpu.SMEM),
        ],
        out_specs=pl.BlockSpec((ATT_QB, BRANCH, CHUNK), lambda bi, i: (bi * ni + i, 0, 0)),
        out_shape=jax.ShapeDtypeStruct((b * nb, BRANCH, CHUNK), F32),
        compiler_params=_params(2),
        name="att",
    )(small3, small3, small3, p_t, p_t, p_t, p_t, rope_rows, rope_rows, rope_rows, rope_t,
      qg_t, kg, bdk, sink)


def _scan_masks(reverse):
    r = lax.broadcasted_iota(jnp.int32, (CHUNK, CHUNK), 0)
    c = lax.broadcasted_iota(jnp.int32, (CHUNK, CHUNK), 1)
    causal = (r <= c) if reverse else (r >= c)
    anti = (r >= c) if reverse else (r <= c)
    return causal, anti


def _scan_block(reverse, n):
    return (lambda c: n - 1 - c) if reverse else (lambda c: c)


AUG = MLSTM_DK + BF16_ROWS
LOG_QK_SCALE = float(np.log(MLSTM_DK ** -0.5))


def _mlstm_kernel(reverse, qk_ref, vt_ref, g_ref, dp_ref, *rest):
    if reverse:
        prev_ref, ng_ref, o_ref, st_sc, m_sc = rest
    else:
        o_ref, st_sc, m_sc = rest
    nblk = MLSTM_BLOCK
    rows = nblk * SUBLANES

    @pl.when(pl.program_id(1) == 0)
    def _():
        st_sc[...] = jnp.zeros_like(st_sc)
        m_sc[...] = jnp.zeros_like(m_sc)

    _, anti = _scan_masks(reverse)
    g = g_ref[...]
    ig = (g[:, 0:SUBLANES, :] + dp_ref[0:SUBLANES, :]).reshape(rows, CHUNK)
    lf = _log_sigmoid(g[:, SUBLANES:, :] + dp_ref[SUBLANES:, :]).reshape(rows, CHUNK)
    b = _cum_right(lf, anti.astype(BF16))
    a = ig - b
    a_max = jnp.max(a, axis=1, keepdims=True)
    b_tot = jnp.sum(lf, axis=1, keepdims=True)
    a_col = jnp.concatenate([a * LOG2E, jnp.zeros((CHUNK - rows, CHUNK), F32)], axis=0).T
    ones = jnp.ones((AUG - MLSTM_DK, CHUNK), BF16)
    scale = MLSTM_DK ** -0.5

    m_st = m_sc[...][:, 0:1]
    st = [st_sc[h] for h in range(MLSTM_HEADS)]
    order = range(nblk - 1, -1, -1) if reverse else range(nblk)
    for j in order:
        sl = slice(j * SUBLANES, (j + 1) * SUBLANES)
        tok = slice(j * CHUNK, (j + 1) * CHUNK)
        mm = jnp.maximum(m_st, a_max[sl])
        wk = jnp.exp(a[sl] - mm) * scale
        decay = jnp.exp(m_st - mm)
        ks, vts, sts = [], [], []
        for h in range(MLSTM_HEADS):
            q = qk_ref[tok, h * MLSTM_DK:(h + 1) * MLSTM_DK]
            k = qk_ref[tok, BRANCH + h * MLSTM_DK:BRANCH + (h + 1) * MLSTM_DK]
            ks.append(k)
            vts.append(jnp.concatenate([vt_ref[j, h * MLSTM_DK:(h + 1) * MLSTM_DK, :], ones], axis=0))
            sts.append(_dot_nt(jnp.concatenate([k, st[h].astype(BF16)], axis=0), q))
        scs, w_inters, m_ts = [], [], []
        for h in range(MLSTM_HEADS):
            col = j * SUBLANES + h
            a_m = jnp.where(anti, jnp.broadcast_to(a_col[:, col:col + 1], (CHUNK, CHUNK)), NEG)
            m_h = m_st[h:h + 1, :]
            u2 = jnp.maximum(m_h * LOG2E, jnp.max(a_m, axis=0, keepdims=True))
            u = u2 * (1.0 / LOG2E)
            m_ts.append(b[col:col + 1, :] + u)
            w_inters.append(jnp.exp(m_h - u))
            scs.append((sts[h][:CHUNK] * jnp.exp2(a_m - (u2 - LOG_QK_SCALE * LOG2E))).astype(BF16))
        outs = []
        for h in range(MLSTM_HEADS):
            nd = _dot(vts[h], scs[h]) + w_inters[h] * sts[h][CHUNK:]
            den = jnp.maximum(jnp.abs(nd[MLSTM_DK:MLSTM_DK + 1, :]), jnp.exp(-m_ts[h]))
            outs.append(nd[:MLSTM_DK] * (1.0 / den))
        for h in range(MLSTM_HEADS):
            vw = (vts[h].astype(F32) * wk[h:h + 1, :]).astype(BF16)
            st[h] = decay[h:h + 1, :] * st[h] + _dot(vw, ks[h])
        m_st = b_tot[sl] + mm
        if reverse:
            normed = []
            for h in range(MLSTM_HEADS):
                x = prev_ref[j, h * MLSTM_DK:(h + 1) * MLSTM_DK, :] + outs[h]
                ms = jnp.mean(x * x, axis=0, keepdims=True)
                normed.append(x * lax.rsqrt(ms + RMS_EPS))
            o_ref[j] = jnp.concatenate(normed, axis=0) * ng_ref[...]
        else:
            o_ref[j] = jnp.concatenate(outs, axis=0)
    for h in range(MLSTM_HEADS):
        st_sc[h] = st[h]
    m_sc[...] = jnp.broadcast_to(m_st, (SUBLANES, LANES))


def _mlstm_call(reverse, mqk, p_t, g_t, b, s, dirp, prev=None, ng_t=None):
    nb = s // (CHUNK * MLSTM_BLOCK)
    blk = _scan_block(reverse, nb)
    mqk3 = mqk.reshape(b, s, MQK_W)
    d = int(reverse)
    out_spec = pl.BlockSpec((MLSTM_BLOCK, BRANCH, CHUNK), lambda bi, c: (bi * nb + blk(c), 0, 0))
    in_specs = [
        pl.BlockSpec((None, CHUNK * MLSTM_BLOCK, MQK_W), lambda bi, c: (bi, blk(c), 0)),
        pl.BlockSpec((MLSTM_BLOCK, BRANCH, CHUNK), lambda bi, c: (bi * nb + blk(c), T_MV // BRANCH, 0)),
        pl.BlockSpec((MLSTM_BLOCK, 2 * SUBLANES, CHUNK), lambda bi, c: (bi * nb + blk(c), d, 0)),
        pl.BlockSpec((None, 2 * SUBLANES, LANES), lambda bi, c: (d, 0, 0)),
    ]
    args = [mqk3, p_t, g_t, dirp]
    if reverse:
        in_specs += [out_spec, pl.BlockSpec((BRANCH, LANES), lambda bi, c: (0, 0))]
        args += [prev, ng_t]
    return pl.pallas_call(
        functools.partial(_mlstm_kernel, reverse),
        grid=(b, nb),
        in_specs=in_specs,
        out_specs=out_spec,
        out_shape=jax.ShapeDtypeStruct((b * s // CHUNK, BRANCH, CHUNK), F32),
        scratch_shapes=[
            pltpu.VMEM((MLSTM_HEADS, AUG, MLSTM_DK), F32),
            pltpu.VMEM((SUBLANES, LANES), F32),
        ],
        compiler_params=_params(2),
        name="mlstm_bwd" if reverse else "mlstm_fwd",
    )(*args)


SSD_BLOCK = 4


def _ssd_gates(reverse, g_ref, dp_ref):
    _, anti = _scan_masks(reverse)
    pre = (g_ref[...] + dp_ref[0:SSM_HEADS, :]).reshape(SSD_BLOCK * SSM_HEADS, CHUNK)
    dt = _softplus(pre)
    a_rows = jnp.concatenate([-jnp.exp(dp_ref[SSM_HEADS:, :]) * LOG2E] * SSD_BLOCK, axis=0)
    da = dt * a_rows
    acs = _cum_right(da, anti.astype(BF16))
    acs_r = acs - jnp.log2(dt)
    tot = jnp.sum(da, axis=1, keepdims=True)
    n = SSD_BLOCK * SSM_HEADS
    acs_c = jnp.concatenate([acs, jnp.zeros((CHUNK - n, CHUNK), F32)], axis=0).T
    return acs_c, acs_r, tot


def _ssd_local(reverse, xs_b, bt, cm_b, gates, j_chunk):
    causal, _ = _scan_masks(reverse)
    acs_c, acs_r, tot = gates

    lane = lax.broadcasted_iota(jnp.int32, (CHUNK, LANES), 1)
    lo = lane < 64
    lo_row = lo[0:1, :]
    bt_b = bt.astype(BF16)
    zero_b = jnp.zeros((CHUNK, LANES), BF16)
    cb = [_dot(jnp.where(lo, cm_b, zero_b), bt_b), _dot(jnp.where(lo, zero_b, cm_b), bt_b)]

    out = []
    for j in range(4):
        grp = j // 2
        m_parts, bw_parts, eac, edec = [], [], [], []
        for e in (2 * j, 2 * j + 1):
            r = j_chunk * SSM_HEADS + e
            ac = acs_c[:, r:r + 1]
            ar = acs_r[r:r + 1, :]
            te = tot[r:r + 1, :]
            m_parts.append((cb[grp] * jnp.exp2(jnp.where(causal, ac - ar, NEG))).astype(BF16))
            bw_parts.append((bt * jnp.exp2(te - ar)).astype(BF16))
            eac.append(jnp.exp2(ac))
            edec.append(jnp.exp2(te))
        lhs = jnp.concatenate([jnp.concatenate(m_parts, axis=1),
                               jnp.concatenate(bw_parts, axis=1)], axis=0)
        xp = xs_b[:, j * LANES:(j + 1) * LANES]
        x2 = jnp.concatenate([jnp.where(lo, xp, zero_b), jnp.where(lo, zero_b, xp)], axis=0)
        out.append((_dot(lhs, x2), jnp.where(lo, eac[0], eac[1]), jnp.where(lo_row, edec[0], edec[1])))
    return out


def _ssd_carry(local, cm_b, st):
    row = lax.broadcasted_iota(jnp.int32, (CHUNK, LANES), 0)
    ys = []
    for j in range(4):
        res, eac, edec = local[j]
        grp = j // 2
        in_grp = (row >= 64 * grp) & (row < 64 * grp + 64)
        ys.append(res[:CHUNK] + _dot(cm_b, st[j].astype(BF16)) * eac)
        st[j] = st[j] * edec + jnp.where(in_grp, res[CHUNK:], 0.0)
    return jnp.concatenate(ys, axis=1)


def _ssd_fwd_kernel(x_ref, xp_ref, xn_ref, g_ref, dp_ref, cw_ref, cb_ref, dsk_ref, y_ref, xc_ref,
                    st_sc, pad_sc):
    c = pl.program_id(1)
    nc = pl.num_programs(1)
    rows = SSD_BLOCK * CHUNK

    @pl.when(c == 0)
    def _():
        st_sc[...] = jnp.zeros_like(st_sc)

    pad_sc[0:HALO, :] = jnp.where(c > 0, xp_ref[...].astype(F32), 0.0)
    pad_sc[HALO:HALO + rows, :] = x_ref[...].astype(F32)
    pad_sc[HALO + rows:, :] = jnp.where(c < nc - 1, xn_ref[...].astype(F32), 0.0)
    pad = pad_sc[...]
    n_pad = rows + 2 * HALO
    mid = SSM_CONV // 2
    shifted = [pad if t == mid else pltpu.roll(pad, (n_pad - (t - mid)) % n_pad, 0) for t in range(SSM_CONV)]
    gates = _ssd_gates(False, g_ref, dp_ref)
    xcs, local = [], []
    for j in range(SSD_BLOCK):
        tok = slice(j * CHUNK, (j + 1) * CHUNK)
        acc = jnp.broadcast_to(cb_ref[...], (CHUNK, XBC_W))
        for t in range(SSM_CONV):
            acc = acc + shifted[t][HALO + j * CHUNK:HALO + (j + 1) * CHUNK, :] * cw_ref[t:t + 1, :]
        xc = _silu(acc)
        xc_b = xc.astype(BF16)
        xc_ref[tok, :] = xc_b
        xcs.append((xc, xc_b))
        local.append(_ssd_local(False, xc_b[:, :BRANCH], xc[:, BRANCH:BRANCH + LANES].T,
                                xc_b[:, BRANCH + LANES:], gates, j))
    st = [st_sc[j] for j in range(4)]
    for j in range(SSD_BLOCK):
        xc, xc_b = xcs[j]
        y = _ssd_carry(local[j], xc_b[:, BRANCH + LANES:], st)
        y_ref[j * CHUNK:(j + 1) * CHUNK, :] = y + dsk_ref[...] * xc[:, :BRANCH]
    for j in range(4):
        st_sc[j] = st[j]


def _ssd_bwd_kernel(xc_ref, g_ref, dp_ref, prev_ref, y_ref, st_sc):
    @pl.when(pl.program_id(1) == 0)
    def _():
        st_sc[...] = jnp.zeros_like(st_sc)

    order = range(SSD_BLOCK - 1, -1, -1)
    gates = _ssd_gates(True, g_ref, dp_ref)
    local = {}
    for j in order:
        tok = slice(j * CHUNK, (j + 1) * CHUNK)
        bt = xc_ref[tok, BRANCH:BRANCH + LANES].astype(F32).T
        local[j] = _ssd_local(True, xc_ref[tok, :BRANCH], bt, xc_ref[tok, BRANCH + LANES:], gates, j)
    st = [st_sc[j] for j in range(4)]
    for j in order:
        tok = slice(j * CHUNK, (j + 1) * CHUNK)
        y_ref[tok, :] = prev_ref[tok, :] + _ssd_carry(local[j], xc_ref[tok, BRANCH + LANES:], st)
    for j in range(4):
        st_sc[j] = st[j]


def _ssd_calls(xbc, g_t, b, s, dirp, conv_w8, conv_b, dskip):
    rows = SSD_BLOCK * CHUNK
    nb = s // rows
    per = rows // HALO
    last = s // HALO - 1
    xbc3 = xbc.reshape(b, s, XBC_W)
    state = pltpu.VMEM((4, CHUNK, LANES), F32)
    y_shape = jax.ShapeDtypeStruct((b, s, BRANCH), F32)
    const = lambda shape: pl.BlockSpec(shape, lambda bi, c: (0,) * len(shape))

    def block_spec(width, blk, col=0):
        return pl.BlockSpec((None, rows, width), lambda bi, c: (bi, blk(c), col))

    fwd = _scan_block(False, nb)
    y_fwd, xc = pl.pallas_call(
        _ssd_fwd_kernel,
        grid=(b, nb),
        in_specs=[
            block_spec(XBC_W, fwd),
            pl.BlockSpec((None, HALO, XBC_W), lambda bi, c: (bi, jnp.maximum(c * per - 1, 0), 0)),
            pl.BlockSpec((None, HALO, XBC_W), lambda bi, c: (bi, jnp.minimum(c * per + per, last), 0)),
            pl.BlockSpec((SSD_BLOCK, SSM_HEADS, CHUNK), lambda bi, c: (bi * nb + fwd(c), G_SSD // SSM_HEADS, 0)),
            pl.BlockSpec((None, 2 * SSM_HEADS, LANES), lambda bi, c: (0, 0, 0)),
            const((SUBLANES, XBC_W)), const((1, XBC_W)), const((1, BRANCH)),
        ],
        out_specs=[block_spec(BRANCH, fwd), block_spec(XBC_W, fwd)],
        out_shape=[y_shape, jax.ShapeDtypeStruct((b, s, XBC_W), BF16)],
        scratch_shapes=[state, pltpu.VMEM((rows + 2 * HALO, XBC_W), F32)],
        compiler_params=_params(2),
        name="ssd_fwd",
    )(xbc3, xbc3, xbc3, g_t, dirp, conv_w8, conv_b, dskip)

    bwd = _scan_block(True, nb)
    return pl.pallas_call(
        _ssd_bwd_kernel,
        grid=(b, nb),
        in_specs=[
            block_spec(XBC_W, bwd),
            pl.BlockSpec((SSD_BLOCK, SSM_HEADS, CHUNK), lambda bi, c: (bi * nb + bwd(c), G_SSD // SSM_HEADS + 1, 0)),
            pl.BlockSpec((None, 2 * SSM_HEADS, LANES), lambda bi, c: (1, 0, 0)),
            block_spec(BRANCH, bwd),
        ],
        out_specs=block_spec(BRANCH, bwd),
        out_shape=y_shape,
        scratch_shapes=[state],
        compiler_params=_params(2),
        name="ssd_bwd",
    )(xc, g_t, dirp, y_fwd)


def _final_kernel(x_ref, yat_ref, hmt_ref, ys_ref, ng_ref, sg_ref, w2_ref, wa_ref, wb_ref, wc_ref,
                  wo_ref, o_ref):
    x = x_ref[...]
    h = _rms(x, ng_ref[...]).astype(BF16)

    def proj(k):
        return _dot(h, w2_ref[:, k * BRANCH:(k + 1) * BRANCH])

    def gate(k):
        lo = 4 * BRANCH + k * D_MODEL
        return _sigmoid(_dot(h, w2_ref[:, lo:lo + D_MODEL]))

    def token_major(ref):
        return jnp.concatenate([ref[j].T for j in range(FINAL_TM // CHUNK)], axis=0)

    ya = token_major(yat_ref) * _silu(proj(0))
    merged = gate(0) * _dot(ya.astype(BF16), wa_ref[...])
    yb = token_major(hmt_ref) * _sigmoid(proj(1)) * _silu(proj(2))
    merged = merged + gate(1) * _dot(yb.astype(BF16), wb_ref[...])
    yc = _rms(ys_ref[...] * _silu(proj(3)), sg_ref[...])
    merged = merged + gate(2) * _dot(yc.astype(BF16), wc_ref[...])
    o_ref[...] = x + _dot(merged.astype(BF16), wo_ref[...])


def _final_call(x2, ya_t, hm_t, ys, ng, sg, w2, wa, wb, wc, wo):
    t = x2.shape[0]
    tm = FINAL_TM
    row = lambda width: pl.BlockSpec((tm, width), lambda i: (i, 0))
    chunked = pl.BlockSpec((tm // CHUNK, BRANCH, CHUNK), lambda i: (i, 0, 0))
    full = lambda shape: pl.BlockSpec(shape, lambda i: (0, 0), pipeline_mode=pl.Buffered(1))
    return pl.pallas_call(
        _final_kernel,
        grid=(t // tm,),
        in_specs=[
            row(D_MODEL), chunked, chunked, row(BRANCH),
            full((1, D_MODEL)), full((1, BRANCH)),
            full((D_MODEL, W2_WIDTH)), full((BRANCH, D_MODEL)), full((BRANCH, D_MODEL)),
            full((BRANCH, D_MODEL)), full((D_MODEL, D_MODEL)),
        ],
        out_specs=row(D_MODEL),
        out_shape=jax.ShapeDtypeStruct((t, D_MODEL), F32),
        compiler_params=_params(1),
        name="final",
    )(x2, ya_t, hm_t, ys, ng, sg, w2, wa, wb, wc, wo)


_IN_SPLITS = (512, 128, 128, 512, 512, 512, 512, 512, 8, 8, 512, 768, 16, 512, 3072)
_NAMES = ("aq", "ak", "av", "az", "mq", "mk", "mv", "mo", "mi", "mf", "mz", "sxbc", "sdt", "sz", "gates")


def _split_w_in(w):
    bounds = np.cumsum((0,) + _IN_SPLITS)
    return {n: w[:, bounds[i]:bounds[i + 1]] for i, n in enumerate(_NAMES)}


def _layer_params(w_in, q_norm_g, k_norm_g, mlstm_i_b, mlstm_f_b, mlstm_norm_g, a_log, dt_bias, conv_w,
                  conv_b, d_skip):
    p = _split_w_in(w_in)
    zc = lambda n: jnp.zeros((D_MODEL, n), F32)
    w_row = jnp.concatenate([p["mq"], p["mk"], p["sxbc"], p["ak"]], axis=1).astype(BF16)
    w_t = jnp.concatenate([p["mv"], p["aq"], p["av"]], axis=1).T.astype(BF16)
    m_gates = []
    for d in range(2):
        m_gates += [p["mi"][:, 4 * d:4 * d + 4], zc(4), p["mf"][:, 4 * d:4 * d + 4], zc(4)]
    w_g = jnp.concatenate(m_gates + [p["sdt"]], axis=1).T.astype(BF16)
    w2 = jnp.concatenate([p["az"], p["mo"], p["mz"], p["sz"], p["gates"]], axis=1).astype(BF16)
    sdirp = jnp.broadcast_to(jnp.concatenate([dt_bias, a_log], axis=1)[:, :, None], (2, 2 * SSM_HEADS, LANES))
    mdirp = jnp.zeros((2, 2 * SUBLANES, LANES), F32)
    mdirp = mdirp.at[:, 0:4, :].set(mlstm_i_b[:, :, None]).at[:, 8:12, :].set(mlstm_f_b[:, :, None])
    return dict(
        w_row=w_row, w_t=w_t, w_g=w_g, w2=w2, sdirp=sdirp, mdirp=mdirp,
        qg_t=jnp.broadcast_to(q_norm_g[:, None], (ATT_HEAD_DIM, LANES)),
        kg=jnp.tile(k_norm_g, 2)[None, :],
        ng_t=jnp.broadcast_to(mlstm_norm_g[:, None], (BRANCH, LANES)),
        conv_w8=jnp.zeros((SUBLANES, XBC_W), F32).at[:SSM_CONV].set(conv_w), conv_b=conv_b[None, :],
        dskip=jnp.repeat(d_skip, 64)[None, :])


def _rope_tables(s_len):
    inv_freq = ROPE_THETA ** (-jnp.arange(ROT_HALF, dtype=F32) * 2.0 / (2 * ROT_HALF))
    ang = jnp.arange(s_len).astype(F32)[:, None] * inv_freq
    cos, sin = jnp.cos(ang), jnp.sin(ang)
    rest = ATT_HEAD_DIM - 2 * ROT_HALF
    one = jnp.ones((s_len, rest), F32)
    zer = jnp.zeros((s_len, rest), F32)
    z8 = jnp.zeros((s_len, ROT_HALF), F32)
    tc = jnp.concatenate([cos, cos, one], axis=1)
    t1 = jnp.concatenate([-sin, z8, zer], axis=1)
    t2 = jnp.concatenate([z8, sin, zer], axis=1)
    rows = jnp.concatenate([jnp.tile(t, (1, 2)) for t in (tc, t1, t2)], axis=1)
    nb = s_len // CHUNK
    t_tab = jnp.concatenate([cos.T.reshape(ROT_HALF, nb, CHUNK), sin.T.reshape(ROT_HALF, nb, CHUNK)], axis=0)
    return rows, t_tab.transpose(1, 0, 2)


def _block_diag_mean(width):
    blk = np.kron(np.eye(width // ATT_HEAD_DIM), np.full((ATT_HEAD_DIM, ATT_HEAD_DIM), 1.0 / ATT_HEAD_DIM))
    return jnp.asarray(blk, BF16)


def _layer(x, lp, ropes, bdk, norm_g, attn_sink, ssm_norm_g, wa, wb, wc, wo):
    b, s, _ = x.shape
    assert s % (CHUNK * max(MLSTM_BLOCK, SSD_BLOCK, ATT_QB)) == 0 and (b * s) % max(PROJ_TM, FINAL_TM) == 0, (b, s)
    x2 = x.reshape(b * s, D_MODEL)
    mqk, xbc, small, p_t, g_t = _proj_call(x2, norm_g[None, :], lp["w_row"], lp["w_t"], lp["w_g"])
    ya_t = _att_call(small, p_t, b, s, ropes[0], ropes[1], lp["qg_t"], lp["kg"], bdk, attn_sink)
    hm_fwd = _mlstm_call(False, mqk, p_t, g_t, b, s, lp["mdirp"])
    hm_t = _mlstm_call(True, mqk, p_t, g_t, b, s, lp["mdirp"], hm_fwd, lp["ng_t"])
    ys = _ssd_calls(xbc, g_t, b, s, lp["sdirp"], lp["conv_w8"], lp["conv_b"], lp["dskip"])
    out = _final_call(x2, ya_t, hm_t, ys.reshape(b * s, BRANCH), norm_g[None, :], ssm_norm_g[None, :],
                      lp["w2"], wa, wb, wc, wo)
    return out.reshape(b, s, D_MODEL)


def kernel(x_prompt, x_sample, norm_g, w_in, q_norm_g, k_norm_g, attn_sink, w_att_out, mlstm_i_b, mlstm_f_b,
           mlstm_norm_g, w_mlstm_out, conv_w, conv_b, a_log, dt_bias, d_skip, ssm_norm_g, w_ssm_out, w_out):
    depth = w_in.shape[0]
    bdk = _block_diag_mean(LANES)
    ropes = {x.shape[1]: _rope_tables(x.shape[1]) for x in (x_prompt, x_sample)}
    ys = [x_prompt, x_sample]
    for l in range(depth):
        lp = _layer_params(w_in[l], q_norm_g[l], k_norm_g[l], mlstm_i_b[l], mlstm_f_b[l], mlstm_norm_g[l],
                           a_log[l], dt_bias[l], conv_w[l], conv_b[l], d_skip[l])
        wa, wb, wc, wo = (w.astype(BF16) for w in (w_att_out[l], w_mlstm_out[l], w_ssm_out[l], w_out[l]))
        ys = [_layer(y, lp, ropes[y.shape[1]], bdk, norm_g[l], attn_sink[l], ssm_norm_g[l], wa, wb, wc, wo)
              for y in ys]
    return tuple(ys)
```

```python
import functools

import jax
import jax.numpy as jnp
import numpy as np
from jax import lax
from jax.experimental import pallas as pl
from jax.experimental.pallas import tpu as pltpu

F32 = jnp.float32
BF16 = jnp.bfloat16

D_MODEL = 1024
BRANCH = 512
RMS_EPS = 1e-6
CHUNK = 128
LANES = 128
SUBLANES = 8
BF16_ROWS = 16
ATT_HEADS = 8
ATT_HEAD_DIM = 64
ROT_HALF = 8
ROPE_THETA = 500000.0
MLSTM_HEADS = 4
MLSTM_DK = 128
MLSTM_BLOCK = 8
SSM_HEADS = 8
SSM_CONV = 5
HALO = BF16_ROWS
NEG = -1e30
LOG2E = float(np.log2(np.e))

MQK_W = 1024
XBC_W = 768
SMALL_W = 128
ROW_W = MQK_W + XBC_W + SMALL_W
T_MV, T_AQ, T_AV = 0, 512, 1024
T_ROWS = 1152
G_ROWS = 48
G_SSD = 32
W2_WIDTH = 5120
PROJ_TM = 1024
FINAL_TM = 512

VMEM_LIMIT = 56 * 1024 * 1024


def _dot(a, b):
    return jnp.dot(a, b, preferred_element_type=F32)


def _dot_nt(a, b):
    return lax.dot_general(a, b, (((1,), (1,)), ((), ())), preferred_element_type=F32)


def _split3(x):
    h1 = x.astype(BF16)
    r1 = x - h1.astype(F32)
    h2 = r1.astype(BF16)
    r2 = r1 - h2.astype(F32)
    return h1, h2, r2.astype(BF16)


def _cum_right(x, mask_bf):
    h1, h2, h3 = _split3(x)
    return _dot(h1, mask_bf) + _dot(h2, mask_bf) + _dot(h3, mask_bf)


def _sigmoid_of_twice(h):
    return 0.5 * jnp.tanh(h) + 0.5


def _silu_of_twice(h):
    return h * jnp.tanh(h) + h


def _softplus(x):
    return jnp.maximum(x, 0.0) + jnp.log1p(jnp.exp(-jnp.abs(x)))


def _log_sigmoid(x):
    return -_softplus(-x)


def _rms(x, g_row):
    ms = jnp.mean(x * x, axis=-1, keepdims=True)
    return x * lax.rsqrt(ms + RMS_EPS) * g_row


def _params(n_axes):
    return pltpu.CompilerParams(dimension_semantics=("arbitrary",) * n_axes, vmem_limit_bytes=VMEM_LIMIT)


def _proj_kernel(x_ref, ng_ref, wr_ref, wt_ref, wg_ref, mqk_ref, xbc_ref, small_ref, t_ref, g_ref):
    h = _rms(x_ref[...], ng_ref[...]).astype(BF16)
    for j0 in range(0, MQK_W, 512):
        mqk_ref[:, j0:j0 + 512] = _dot(h, wr_ref[:, j0:j0 + 512]).astype(BF16)
    xbc_ref[...] = _dot(h, wr_ref[:, MQK_W:MQK_W + XBC_W]).astype(BF16)
    small_ref[...] = _dot(h, wr_ref[:, MQK_W + XBC_W:])
    n_chunks = PROJ_TM // CHUNK
    for r0 in range(0, T_ROWS, 384):
        res = _dot_nt(wt_ref[r0:r0 + 384, :], h).astype(BF16)
        for j in range(n_chunks):
            t_ref[j, r0:r0 + 384, :] = res[:, j * CHUNK:(j + 1) * CHUNK]
    gates = _dot_nt(wg_ref[...], h)
    for j in range(n_chunks):
        g_ref[j] = gates[:, j * CHUNK:(j + 1) * CHUNK]


def _proj_call(x2, ng, w_row, w_t, w_g):
    t = x2.shape[0]
    tm = PROJ_TM
    full = lambda shape: pl.BlockSpec(shape, lambda i: (0, 0), pipeline_mode=pl.Buffered(1))
    rows = lambda width: pl.BlockSpec((tm, width), lambda i: (i, 0))
    chunks = lambda r: pl.BlockSpec((tm // CHUNK, r, CHUNK), lambda i: (i, 0, 0))
    return pl.pallas_call(
        _proj_kernel,
        grid=(t // tm,),
        in_specs=[rows(D_MODEL), full((1, D_MODEL)), full((D_MODEL, ROW_W)), full((T_ROWS, D_MODEL)),
                  full((G_ROWS, D_MODEL))],
        out_specs=[rows(MQK_W), rows(XBC_W), rows(SMALL_W), chunks(T_ROWS), chunks(G_ROWS)],
        out_shape=[
            jax.ShapeDtypeStruct((t, MQK_W), BF16),
            jax.ShapeDtypeStruct((t, XBC_W), BF16),
            jax.ShapeDtypeStruct((t, SMALL_W), F32),
            jax.ShapeDtypeStruct((t // CHUNK, T_ROWS, CHUNK), BF16),
            jax.ShapeDtypeStruct((t // CHUNK, G_ROWS, CHUNK), F32),
        ],
        compiler_params=_params(1),
        name="proj",
    )(x2, ng, w_row, w_t, w_g)


def _rope_rows(x, tab):
    w = x.shape[1]
    return (x * tab[:, :w]
            + pltpu.roll(x, w - ROT_HALF, 1) * tab[:, w:2 * w]
            + pltpu.roll(x, ROT_HALF, 1) * tab[:, 2 * w:])


ATT_QB = 8


def _att_kernel(kp_ref, kc_ref, kn_ref, qt_ref, vp_ref, vc_ref, vn_ref, rp_ref, rc_ref, rn_ref,
                rt_ref, qg_ref, kg_ref, bdk_ref, sink_ref, o_ref):
    i = pl.program_id(1)
    ni = pl.num_programs(1)
    win = 3 * CHUNK
    nkeys = (ATT_QB + 2) * CHUNK

    k = jnp.concatenate([kp_ref[...], kc_ref[...], kn_ref[...]], axis=0)
    tab_k = jnp.concatenate([rp_ref[...], rc_ref[...], rn_ref[...]], axis=0)
    kms = _dot((k * k).astype(BF16), bdk_ref[...])
    kr = _rope_rows(k * lax.rsqrt(kms + RMS_EPS) * kg_ref[...], tab_k)
    lane = lax.broadcasted_iota(jnp.int32, (nkeys, LANES), 1)
    lo = lane < ATT_HEAD_DIM
    k_sw = pltpu.roll(kr, ATT_HEAD_DIM, 1)
    zero = jnp.zeros_like(kr)
    k_var = [[jnp.where(lo, kr, zero).astype(BF16), jnp.where(lo, zero, k_sw).astype(BF16)],
             [jnp.where(lo, k_sw, zero).astype(BF16), jnp.where(lo, zero, kr).astype(BF16)]]

    v_t = jnp.concatenate([vp_ref[...]] + [vc_ref[j] for j in range(ATT_QB)] + [vn_ref[...]], axis=1)
    ones = jnp.ones((BF16_ROWS, win), BF16)

    s_idx = lax.broadcasted_iota(jnp.int32, (win, 2 * CHUNK), 0)
    t_idx = lax.broadcasted_iota(jnp.int32, (win, 2 * CHUNK), 1) % CHUNK
    band = (s_idx >= t_idx) & (s_idx <= t_idx + 2 * CHUNK)
    first_head = lax.broadcasted_iota(jnp.int32, (1, 2 * CHUNK), 1) < CHUNK

    s2 = []
    for qb in range(ATT_QB):
        cos = rt_ref[qb, 0:ROT_HALF, :]
        sin = rt_ref[qb, ROT_HALF:, :]
        q_heads = []
        for h in range(ATT_HEADS):
            x = qt_ref[qb, h * ATT_HEAD_DIM:(h + 1) * ATT_HEAD_DIM, :].astype(F32)
            ms = jnp.mean(x * x, axis=0, keepdims=True)
            xn = x * lax.rsqrt(ms + RMS_EPS) * qg_ref[...]
            x1 = xn[0:ROT_HALF]
            x2 = xn[ROT_HALF:2 * ROT_HALF]
            xr = jnp.concatenate([x1 * cos - x2 * sin, x2 * cos + x1 * sin, xn[2 * ROT_HALF:]], axis=0)
            q_heads.append((xr * (ATT_HEAD_DIM ** -0.5 * LOG2E)).astype(BF16))
        keys = slice(qb * CHUNK, qb * CHUNK + win)
        for g in range(2):
            lhs = jnp.concatenate([k_var[g][0][keys], k_var[g][1][keys]], axis=0)
            rhs = jnp.concatenate(
                [jnp.concatenate([q_heads[4 * g + 2 * j], q_heads[4 * g + 2 * j + 1]], axis=0)
                 for j in range(2)], axis=1)
            s2.append(_dot(lhs, rhs))
    ps, dens = [], []
    for qb in range(ATT_QB):
        valid = band
        if qb == 0:
            valid = valid & ((s_idx >= CHUNK) | (i > 0))
        if qb == ATT_QB - 1:
            valid = valid & ((s_idx < 2 * CHUNK) | (i < ni - 1))
        bias = jnp.where(valid, 0.0, NEG)
        for g in range(2):
            for half in range(2):
                s = s2[2 * qb + g][half * win:(half + 1) * win] + bias
                sink = jnp.where(first_head, sink_ref[4 * g + half], sink_ref[4 * g + 2 + half]) * LOG2E
                mx = jnp.maximum(jnp.max(s, axis=0, keepdims=True), sink)
                p = jnp.exp2(s - mx)
                dens.append(jnp.exp2(sink - mx))
                ps.append(p.astype(BF16))
    for qb in range(ATT_QB):
        outs = [None] * ATT_HEADS
        for g in range(2):
            vg = jnp.concatenate([v_t[g * ATT_HEAD_DIM:(g + 1) * ATT_HEAD_DIM, qb * CHUNK:qb * CHUNK + win], ones], axis=0)
            for half in range(2):
                n = 4 * qb + 2 * g + half
                od = _dot(vg, ps[n])
                o = od[:ATT_HEAD_DIM] * (1.0 / (od[ATT_HEAD_DIM:ATT_HEAD_DIM + 1] + dens[n]))
                outs[4 * g + half] = o[:, :CHUNK]
                outs[4 * g + 2 + half] = o[:, CHUNK:]
        o_ref[qb] = jnp.concatenate(outs, axis=0)


def _att_call(small, p_t, b, s, rope_rows, rope_t, qg_t, kg, bdk, sink):
    nb = s // CHUNK
    ni = nb // ATT_QB
    small3 = small.reshape(b, s, SMALL_W)
    prev = lambda i: jnp.maximum(ATT_QB * i - 1, 0)
    nxt = lambda i: jnp.minimum(ATT_QB * i + ATT_QB, nb - 1)
    av = T_AV // LANES
    return pl.pallas_call(
        _att_kernel,
        grid=(b, ni),
        in_specs=[
            pl.BlockSpec((None, CHUNK, LANES), lambda bi, i: (bi, prev(i), 0)),
            pl.BlockSpec((None, ATT_QB * CHUNK, LANES), lambda bi, i: (bi, i, 0)),
            pl.BlockSpec((None, CHUNK, LANES), lambda bi, i: (bi, nxt(i), 0)),
            pl.BlockSpec((ATT_QB, BRANCH, CHUNK), lambda bi, i: (bi * ni + i, T_AQ // BRANCH, 0)),
            pl.BlockSpec((None, LANES, CHUNK), lambda bi, i: (bi * nb + prev(i), av, 0)),
            pl.BlockSpec((ATT_QB, LANES, CHUNK), lambda bi, i: (bi * ni + i, av, 0)),
            pl.BlockSpec((None, LANES, CHUNK), lambda bi, i: (bi * nb + nxt(i), av, 0)),
            pl.BlockSpec((CHUNK, 3 * LANES), lambda bi, i: (prev(i), 0)),
            pl.BlockSpec((ATT_QB * CHUNK, 3 * LANES), lambda bi, i: (i, 0)),
            pl.BlockSpec((CHUNK, 3 * LANES), lambda bi, i: (nxt(i), 0)),
            pl.BlockSpec((ATT_QB, 2 * ROT_HALF, CHUNK), lambda bi, i: (i, 0, 0)),
            pl.BlockSpec((ATT_HEAD_DIM, LANES), lambda bi, i: (0, 0)),
            pl.BlockSpec((1, LANES), lambda bi, i: (0, 0)),
            pl.BlockSpec((LANES, LANES), lambda bi, i: (0, 0)),
            pl.BlockSpec(memory_space=pltpu.SMEM),
        ],
        out_specs=pl.BlockSpec((ATT_QB, BRANCH, CHUNK), lambda bi, i: (bi * ni + i, 0, 0)),
        out_shape=jax.ShapeDtypeStruct((b * nb, BRANCH, CHUNK), F32),
        compiler_params=_params(2),
        name="att",
    )(small3, small3, small3, p_t, p_t, p_t, p_t, rope_rows, rope_rows, rope_rows, rope_t,
      qg_t, kg, bdk, sink)


def _scan_masks(reverse):
    r = lax.broadcasted_iota(jnp.int32, (CHUNK, CHUNK), 0)
    c = lax.broadcasted_iota(jnp.int32, (CHUNK, CHUNK), 1)
    causal = (r <= c) if reverse else (r >= c)
    anti = (r >= c) if reverse else (r <= c)
    return causal, anti


def _scan_block(reverse, n):
    return (lambda c: n - 1 - c) if reverse else (lambda c: c)


AUG = MLSTM_DK + BF16_ROWS
LOG_QK_SCALE = float(np.log(MLSTM_DK ** -0.5))


def _mlstm_kernel(reverse, qk_ref, vt_ref, g_ref, dp_ref, *rest):
    if reverse:
        prev_ref, ng_ref, o_ref, st_sc, m_sc = rest
    else:
        o_ref, st_sc, m_sc = rest
    nblk = MLSTM_BLOCK
    rows = nblk * SUBLANES

    @pl.when(pl.program_id(1) == 0)
    def _():
        st_sc[...] = jnp.zeros_like(st_sc)
        m_sc[...] = jnp.zeros_like(m_sc)

    _, anti = _scan_masks(reverse)
    g = g_ref[...]
    ig = (g[:, 0:SUBLANES, :] + dp_ref[0:SUBLANES, :]).reshape(rows, CHUNK)
    lf = _log_sigmoid(g[:, SUBLANES:, :] + dp_ref[SUBLANES:, :]).reshape(rows, CHUNK)
    b = _cum_right(lf, anti.astype(BF16))
    a = ig - b
    a_max = jnp.max(a, axis=1, keepdims=True)
    b_tot = jnp.sum(lf, axis=1, keepdims=True)
    a_col = jnp.concatenate([a * LOG2E, jnp.zeros((CHUNK - rows, CHUNK), F32)], axis=0).T
    ones = jnp.ones((AUG - MLSTM_DK, CHUNK), BF16)
    scale = MLSTM_DK ** -0.5

    m_st = m_sc[...][:, 0:1]
    st = [st_sc[h] for h in range(MLSTM_HEADS)]
    order = range(nblk - 1, -1, -1) if reverse else range(nblk)
    for j in order:
        sl = slice(j * SUBLANES, (j + 1) * SUBLANES)
        tok = slice(j * CHUNK, (j + 1) * CHUNK)
        mm = jnp.maximum(m_st, a_max[sl])
        wk = jnp.exp(a[sl] - mm) * scale
        decay = jnp.exp(m_st - mm)
        ks, vts, sts = [], [], []
        for h in range(MLSTM_HEADS):
            q = qk_ref[tok, h * MLSTM_DK:(h + 1) * MLSTM_DK]
            k = qk_ref[tok, BRANCH + h * MLSTM_DK:BRANCH + (h + 1) * MLSTM_DK]
            ks.append(k)
            vts.append(jnp.concatenate([vt_ref[j, h * MLSTM_DK:(h + 1) * MLSTM_DK, :], ones], axis=0))
            sts.append(_dot_nt(jnp.concatenate([k, st[h].astype(BF16)], axis=0), q))
        scs, w_inters, m_ts = [], [], []
        for h in range(MLSTM_HEADS):
            col = j * SUBLANES + h
            a_m = jnp.where(anti, jnp.broadcast_to(a_col[:, col:col + 1], (CHUNK, CHUNK)), NEG)
            m_h = m_st[h:h + 1, :]
            u2 = jnp.maximum(m_h * LOG2E, jnp.max(a_m, axis=0, keepdims=True))
            u = u2 * (1.0 / LOG2E)
            m_ts.append(b[col:col + 1, :] + u)
            w_inters.append(jnp.exp(m_h - u))
            scs.append((sts[h][:CHUNK] * jnp.exp2(a_m - (u2 - LOG_QK_SCALE * LOG2E))).astype(BF16))
        outs = []
        for h in range(MLSTM_HEADS):
            nd = _dot(vts[h], scs[h]) + w_inters[h] * sts[h][CHUNK:]
            den = jnp.maximum(jnp.abs(nd[MLSTM_DK:MLSTM_DK + 1, :]), jnp.exp(-m_ts[h]))
            outs.append(nd[:MLSTM_DK] * (1.0 / den))
        for h in range(MLSTM_HEADS):
            vw = (vts[h].astype(F32) * wk[h:h + 1, :]).astype(BF16)
            st[h] = decay[h:h + 1, :] * st[h] + _dot(vw, ks[h])
        m_st = b_tot[sl] + mm
        if reverse:
            normed = []
            for h in range(MLSTM_HEADS):
                x = prev_ref[j, h * MLSTM_DK:(h + 1) * MLSTM_DK, :] + outs[h]
                ms = jnp.mean(x * x, axis=0, keepdims=True)
                normed.append(x * lax.rsqrt(ms + RMS_EPS))
            o_ref[j] = jnp.concatenate(normed, axis=0) * ng_ref[...]
        else:
            o_ref[j] = jnp.concatenate(outs, axis=0)
    for h in range(MLSTM_HEADS):
        st_sc[h] = st[h]
    m_sc[...] = jnp.broadcast_to(m_st, (SUBLANES, LANES))


def _mlstm_call(reverse, mqk, p_t, g_t, b, s, dirp, prev=None, ng_t=None):
    nb = s // (CHUNK * MLSTM_BLOCK)
    blk = _scan_block(reverse, nb)
    mqk3 = mqk.reshape(b, s, MQK_W)
    d = int(reverse)
    out_spec = pl.BlockSpec((MLSTM_BLOCK, BRANCH, CHUNK), lambda bi, c: (bi * nb + blk(c), 0, 0))
    in_specs = [
        pl.BlockSpec((None, CHUNK * MLSTM_BLOCK, MQK_W), lambda bi, c: (bi, blk(c), 0)),
        pl.BlockSpec((MLSTM_BLOCK, BRANCH, CHUNK), lambda bi, c: (bi * nb + blk(c), T_MV // BRANCH, 0)),
        pl.BlockSpec((MLSTM_BLOCK, 2 * SUBLANES, CHUNK), lambda bi, c: (bi * nb + blk(c), d, 0)),
        pl.BlockSpec((None, 2 * SUBLANES, LANES), lambda bi, c: (d, 0, 0)),
    ]
    args = [mqk3, p_t, g_t, dirp]
    if reverse:
        in_specs += [out_spec, pl.BlockSpec((BRANCH, LANES), lambda bi, c: (0, 0))]
        args += [prev, ng_t]
    return pl.pallas_call(
        functools.partial(_mlstm_kernel, reverse),
        grid=(b, nb),
        in_specs=in_specs,
        out_specs=out_spec,
        out_shape=jax.ShapeDtypeStruct((b * s // CHUNK, BRANCH, CHUNK), F32),
        scratch_shapes=[
            pltpu.VMEM((MLSTM_HEADS, AUG, MLSTM_DK), F32),
            pltpu.VMEM((SUBLANES, LANES), F32),
        ],
        compiler_params=_params(2),
        name="mlstm_bwd" if reverse else "mlstm_fwd",
    )(*args)


SSD_BLOCK = 4


def _ssd_gates(reverse, g_ref, dp_ref):
    _, anti = _scan_masks(reverse)
    pre = (g_ref[...] + dp_ref[0:SSM_HEADS, :]).reshape(SSD_BLOCK * SSM_HEADS, CHUNK)
    dt = _softplus(pre)
    a_rows = jnp.concatenate([-jnp.exp(dp_ref[SSM_HEADS:, :]) * LOG2E] * SSD_BLOCK, axis=0)
    da = dt * a_rows
    acs = _cum_right(da, anti.astype(BF16))
    acs_r = acs - jnp.log2(dt)
    tot = jnp.sum(da, axis=1, keepdims=True)
    n = SSD_BLOCK * SSM_HEADS
    acs_c = jnp.concatenate([acs, jnp.zeros((CHUNK - n, CHUNK), F32)], axis=0).T
    return acs_c, acs_r, tot


def _ssd_local(reverse, xs_b, bt, cm_b, gates, j_chunk):
    causal, _ = _scan_masks(reverse)
    acs_c, acs_r, tot = gates

    lane = lax.broadcasted_iota(jnp.int32, (CHUNK, LANES), 1)
    lo = lane < 64
    lo_row = lo[0:1, :]
    bt_b = bt.astype(BF16)
    zero_b = jnp.zeros((CHUNK, LANES), BF16)
    cb = [_dot(jnp.where(lo, cm_b, zero_b), bt_b), _dot(jnp.where(lo, zero_b, cm_b), bt_b)]

    out = []
    for j in range(4):
        grp = j // 2
        m_parts, bw_parts, eac, edec = [], [], [], []
        for e in (2 * j, 2 * j + 1):
            r = j_chunk * SSM_HEADS + e
            ac = acs_c[:, r:r + 1]
            ar = acs_r[r:r + 1, :]
            te = tot[r:r + 1, :]
            m_parts.append((cb[grp] * jnp.exp2(jnp.where(causal, ac - ar, NEG))).astype(BF16))
            bw_parts.append((bt * jnp.exp2(te - ar)).astype(BF16))
            eac.append(jnp.exp2(ac))
            edec.append(jnp.exp2(te))
        lhs = jnp.concatenate([jnp.concatenate(m_parts, axis=1),
                               jnp.concatenate(bw_parts, axis=1)], axis=0)
        xp = xs_b[:, j * LANES:(j + 1) * LANES]
        x2 = jnp.concatenate([jnp.where(lo, xp, zero_b), jnp.where(lo, zero_b, xp)], axis=0)
        out.append((_dot(lhs, x2), jnp.where(lo, eac[0], eac[1]), jnp.where(lo_row, edec[0], edec[1])))
    return out


def _ssd_carry(local, cm_b, st):
    row = lax.broadcasted_iota(jnp.int32, (CHUNK, LANES), 0)
    ys = []
    for j in range(4):
        res, eac, edec = local[j]
        grp = j // 2
        in_grp = (row >= 64 * grp) & (row < 64 * grp + 64)
        ys.append(res[:CHUNK] + _dot(cm_b, st[j].astype(BF16)) * eac)
        st[j] = st[j] * edec + jnp.where(in_grp, res[CHUNK:], 0.0)
    return jnp.concatenate(ys, axis=1)


def _ssd_fwd_kernel(x_ref, xp_ref, xn_ref, g_ref, dp_ref, cw_ref, cb_ref, dsk_ref, y_ref, xc_ref,
                    st_sc, pad_sc):
    c = pl.program_id(1)
    nc = pl.num_programs(1)
    rows = SSD_BLOCK * CHUNK

    @pl.when(c == 0)
    def _():
        st_sc[...] = jnp.zeros_like(st_sc)

    pad_sc[0:HALO, :] = jnp.where(c > 0, xp_ref[...].astype(F32), 0.0)
    pad_sc[HALO:HALO + rows, :] = x_ref[...].astype(F32)
    pad_sc[HALO + rows:, :] = jnp.where(c < nc - 1, xn_ref[...].astype(F32), 0.0)
    pad = pad_sc[...]
    n_pad = rows + 2 * HALO
    mid = SSM_CONV // 2
    shifted = [pad if t == mid else pltpu.roll(pad, (n_pad - (t - mid)) % n_pad, 0) for t in range(SSM_CONV)]
    gates = _ssd_gates(False, g_ref, dp_ref)
    xcs, local = [], []
    for j in range(SSD_BLOCK):
        tok = slice(j * CHUNK, (j + 1) * CHUNK)
        acc = jnp.broadcast_to(cb_ref[...], (CHUNK, XBC_W))
        for t in range(SSM_CONV):
            acc = acc + shifted[t][HALO + j * CHUNK:HALO + (j + 1) * CHUNK, :] * cw_ref[t:t + 1, :]
        xc = _silu_of_twice(acc)
        xc_b = xc.astype(BF16)
        xc_ref[tok, :] = xc_b
        xcs.append((xc, xc_b))
        local.append(_ssd_local(False, xc_b[:, :BRANCH], xc[:, BRANCH:BRANCH + LANES].T,
                                xc_b[:, BRANCH + LANES:], gates, j))
    st = [st_sc[j] for j in range(4)]
    for j in range(SSD_BLOCK):
        xc, xc_b = xcs[j]
        y = _ssd_carry(local[j], xc_b[:, BRANCH + LANES:], st)
        y_ref[j * CHUNK:(j + 1) * CHUNK, :] = y + dsk_ref[...] * xc[:, :BRANCH]
    for j in range(4):
        st_sc[j] = st[j]


def _ssd_bwd_kernel(xc_ref, g_ref, dp_ref, prev_ref, y_ref, st_sc):
    @pl.when(pl.program_id(1) == 0)
    def _():
        st_sc[...] = jnp.zeros_like(st_sc)

    order = range(SSD_BLOCK - 1, -1, -1)
    gates = _ssd_gates(True, g_ref, dp_ref)
    local = {}
    for j in order:
        tok = slice(j * CHUNK, (j + 1) * CHUNK)
        bt = xc_ref[tok, BRANCH:BRANCH + LANES].astype(F32).T
        local[j] = _ssd_local(True, xc_ref[tok, :BRANCH], bt, xc_ref[tok, BRANCH + LANES:], gates, j)
    st = [st_sc[j] for j in range(4)]
    for j in order:
        tok = slice(j * CHUNK, (j + 1) * CHUNK)
        y_ref[tok, :] = prev_ref[tok, :] + _ssd_carry(local[j], xc_ref[tok, BRANCH + LANES:], st)
    for j in range(4):
        st_sc[j] = st[j]


def _ssd_calls(xbc, g_t, b, s, dirp, conv_w8, conv_b, dskip):
    rows = SSD_BLOCK * CHUNK
    nb = s // rows
    per = rows // HALO
    last = s // HALO - 1
    xbc3 = xbc.reshape(b, s, XBC_W)
    state = pltpu.VMEM((4, CHUNK, LANES), F32)
    y_shape = jax.ShapeDtypeStruct((b, s, BRANCH), F32)
    const = lambda shape: pl.BlockSpec(shape, lambda bi, c: (0,) * len(shape))

    def block_spec(width, blk, col=0):
        return pl.BlockSpec((None, rows, width), lambda bi, c: (bi, blk(c), col))

    fwd = _scan_block(False, nb)
    y_fwd, xc = pl.pallas_call(
        _ssd_fwd_kernel,
        grid=(b, nb),
        in_specs=[
            block_spec(XBC_W, fwd),
            pl.BlockSpec((None, HALO, XBC_W), lambda bi, c: (bi, jnp.maximum(c * per - 1, 0), 0)),
            pl.BlockSpec((None, HALO, XBC_W), lambda bi, c: (bi, jnp.minimum(c * per + per, last), 0)),
            pl.BlockSpec((SSD_BLOCK, SSM_HEADS, CHUNK), lambda bi, c: (bi * nb + fwd(c), G_SSD // SSM_HEADS, 0)),
            pl.BlockSpec((None, 2 * SSM_HEADS, LANES), lambda bi, c: (0, 0, 0)),
            const((SUBLANES, XBC_W)), const((1, XBC_W)), const((1, BRANCH)),
        ],
        out_specs=[block_spec(BRANCH, fwd), block_spec(XBC_W, fwd)],
        out_shape=[y_shape, jax.ShapeDtypeStruct((b, s, XBC_W), BF16)],
        scratch_shapes=[state, pltpu.VMEM((rows + 2 * HALO, XBC_W), F32)],
        compiler_params=_params(2),
        name="ssd_fwd",
    )(xbc3, xbc3, xbc3, g_t, dirp, conv_w8, conv_b, dskip)

    bwd = _scan_block(True, nb)
    return pl.pallas_call(
        _ssd_bwd_kernel,
        grid=(b, nb),
        in_specs=[
            block_spec(XBC_W, bwd),
            pl.BlockSpec((SSD_BLOCK, SSM_HEADS, CHUNK), lambda bi, c: (bi * nb + bwd(c), G_SSD // SSM_HEADS + 1, 0)),
            pl.BlockSpec((None, 2 * SSM_HEADS, LANES), lambda bi, c: (1, 0, 0)),
            block_spec(BRANCH, bwd),
        ],
        out_specs=block_spec(BRANCH, bwd),
        out_shape=y_shape,
        scratch_shapes=[state],
        compiler_params=_params(2),
        name="ssd_bwd",
    )(xc, g_t, dirp, y_fwd)


def _final_kernel(x_ref, yat_ref, hmt_ref, ys_ref, ng_ref, sg_ref, w2_ref, wa_ref, wb_ref, wc_ref,
                  wo_ref, o_ref):
    x = x_ref[...]
    h = _rms(x, ng_ref[...]).astype(BF16)

    def proj(k):
        return _dot(h, w2_ref[:, k * BRANCH:(k + 1) * BRANCH])

    def gate(k):
        lo = 4 * BRANCH + k * D_MODEL
        return _sigmoid_of_twice(_dot(h, w2_ref[:, lo:lo + D_MODEL]))

    def token_major(ref):
        return jnp.concatenate([ref[j].T for j in range(FINAL_TM // CHUNK)], axis=0)

    ya = token_major(yat_ref) * _silu_of_twice(proj(0))
    merged = gate(0) * _dot(ya.astype(BF16), wa_ref[...])
    yb = token_major(hmt_ref) * _sigmoid_of_twice(proj(1)) * _silu_of_twice(proj(2))
    merged = merged + gate(1) * _dot(yb.astype(BF16), wb_ref[...])
    yc = _rms(ys_ref[...] * _silu_of_twice(proj(3)), sg_ref[...])
    merged = merged + gate(2) * _dot(yc.astype(BF16), wc_ref[...])
    o_ref[...] = x + _dot(merged.astype(BF16), wo_ref[...])


def _final_call(x2, ya_t, hm_t, ys, ng, sg, w2, wa, wb, wc, wo):
    t = x2.shape[0]
    tm = FINAL_TM
    row = lambda width: pl.BlockSpec((tm, width), lambda i: (i, 0))
    chunked = pl.BlockSpec((tm // CHUNK, BRANCH, CHUNK), lambda i: (i, 0, 0))
    full = lambda shape: pl.BlockSpec(shape, lambda i: (0, 0), pipeline_mode=pl.Buffered(1))
    return pl.pallas_call(
        _final_kernel,
        grid=(t // tm,),
        in_specs=[
            row(D_MODEL), chunked, chunked, row(BRANCH),
            full((1, D_MODEL)), full((1, BRANCH)),
            full((D_MODEL, W2_WIDTH)), full((BRANCH, D_MODEL)), full((BRANCH, D_MODEL)),
            full((BRANCH, D_MODEL)), full((D_MODEL, D_MODEL)),
        ],
        out_specs=row(D_MODEL),
        out_shape=jax.ShapeDtypeStruct((t, D_MODEL), F32),
        compiler_params=_params(1),
        name="final",
    )(x2, ya_t, hm_t, ys, ng, sg, w2, wa, wb, wc, wo)


_IN_SPLITS = (512, 128, 128, 512, 512, 512, 512, 512, 8, 8, 512, 768, 16, 512, 3072)
_NAMES = ("aq", "ak", "av", "az", "mq", "mk", "mv", "mo", "mi", "mf", "mz", "sxbc", "sdt", "sz", "gates")


def _split_w_in(w):
    bounds = np.cumsum((0,) + _IN_SPLITS)
    return {n: w[:, bounds[i]:bounds[i + 1]] for i, n in enumerate(_NAMES)}


def _layer_params(w_in, q_norm_g, k_norm_g, mlstm_i_b, mlstm_f_b, mlstm_norm_g, a_log, dt_bias, conv_w,
                  conv_b, d_skip):
    p = _split_w_in(w_in)
    zc = lambda n: jnp.zeros((D_MODEL, n), F32)
    w_row = jnp.concatenate([p["mq"], p["mk"], p["sxbc"], p["ak"]], axis=1).astype(BF16)
    w_t = jnp.concatenate([p["mv"], p["aq"], p["av"]], axis=1).T.astype(BF16)
    m_gates = []
    for d in range(2):
        m_gates += [p["mi"][:, 4 * d:4 * d + 4], zc(4), p["mf"][:, 4 * d:4 * d + 4], zc(4)]
    w_g = jnp.concatenate(m_gates + [p["sdt"]], axis=1).T.astype(BF16)
    w2 = (0.5 * jnp.concatenate([p["az"], p["mo"], p["mz"], p["sz"], p["gates"]], axis=1)).astype(BF16)
    sdirp = jnp.broadcast_to(jnp.concatenate([dt_bias, a_log], axis=1)[:, :, None], (2, 2 * SSM_HEADS, LANES))
    mdirp = jnp.zeros((2, 2 * SUBLANES, LANES), F32)
    mdirp = mdirp.at[:, 0:4, :].set(mlstm_i_b[:, :, None]).at[:, 8:12, :].set(mlstm_f_b[:, :, None])
    return dict(
        w_row=w_row, w_t=w_t, w_g=w_g, w2=w2, sdirp=sdirp, mdirp=mdirp,
        qg_t=jnp.broadcast_to(q_norm_g[:, None], (ATT_HEAD_DIM, LANES)),
        kg=jnp.tile(k_norm_g, 2)[None, :],
        ng_t=jnp.broadcast_to(mlstm_norm_g[:, None], (BRANCH, LANES)),
        conv_w8=jnp.zeros((SUBLANES, XBC_W), F32).at[:SSM_CONV].set(0.5 * conv_w), conv_b=0.5 * conv_b[None, :],
        dskip=jnp.repeat(d_skip, 64)[None, :])


def _rope_tables(s_len):
    inv_freq = ROPE_THETA ** (-jnp.arange(ROT_HALF, dtype=F32) * 2.0 / (2 * ROT_HALF))
    ang = jnp.arange(s_len).astype(F32)[:, None] * inv_freq
    cos, sin = jnp.cos(ang), jnp.sin(ang)
    rest = ATT_HEAD_DIM - 2 * ROT_HALF
    one = jnp.ones((s_len, rest), F32)
    zer = jnp.zeros((s_len, rest), F32)
    z8 = jnp.zeros((s_len, ROT_HALF), F32)
    tc = jnp.concatenate([cos, cos, one], axis=1)
    t1 = jnp.concatenate([-sin, z8, zer], axis=1)
    t2 = jnp.concatenate([z8, sin, zer], axis=1)
    rows = jnp.concatenate([jnp.tile(t, (1, 2)) for t in (tc, t1, t2)], axis=1)
    nb = s_len // CHUNK
    t_tab = jnp.concatenate([cos.T.reshape(ROT_HALF, nb, CHUNK), sin.T.reshape(ROT_HALF, nb, CHUNK)], axis=0)
    return rows, t_tab.transpose(1, 0, 2)


def _block_diag_mean(width):
    blk = np.kron(np.eye(width // ATT_HEAD_DIM), np.full((ATT_HEAD_DIM, ATT_HEAD_DIM), 1.0 / ATT_HEAD_DIM))
    return jnp.asarray(blk, BF16)


def _layer(x, lp, ropes, bdk, norm_g, attn_sink, ssm_norm_g, wa, wb, wc, wo):
    b, s, _ = x.shape
    assert s % (CHUNK * max(MLSTM_BLOCK, SSD_BLOCK, ATT_QB)) == 0 and (b * s) % max(PROJ_TM, FINAL_TM) == 0, (b, s)
    x2 = x.reshape(b * s, D_MODEL)
    mqk, xbc, small, p_t, g_t = _proj_call(x2, norm_g[None, :], lp["w_row"], lp["w_t"], lp["w_g"])
    ya_t = _att_call(small, p_t, b, s, ropes[0], ropes[1], lp["qg_t"], lp["kg"], bdk, attn_sink)
    hm_fwd = _mlstm_call(False, mqk, p_t, g_t, b, s, lp["mdirp"])
    hm_t = _mlstm_call(True, mqk, p_t, g_t, b, s, lp["mdirp"], hm_fwd, lp["ng_t"])
    ys = _ssd_calls(xbc, g_t, b, s, lp["sdirp"], lp["conv_w8"], lp["conv_b"], lp["dskip"])
    out = _final_call(x2, ya_t, hm_t, ys.reshape(b * s, BRANCH), norm_g[None, :], ssm_norm_g[None, :],
                      lp["w2"], wa, wb, wc, wo)
    return out.reshape(b, s, D_MODEL)


def kernel(x_prompt, x_sample, norm_g, w_in, q_norm_g, k_norm_g, attn_sink, w_att_out, mlstm_i_b, mlstm_f_b,
           mlstm_norm_g, w_mlstm_out, conv_w, conv_b, a_log, dt_bias, d_skip, ssm_norm_g, w_ssm_out, w_out):
    depth = w_in.shape[0]
    bdk = _block_diag_mean(LANES)
    ropes = {x.shape[1]: _rope_tables(x.shape[1]) for x in (x_prompt, x_sample)}
    ys = [x_prompt, x_sample]
    for l in range(depth):
        lp = _layer_params(w_in[l], q_norm_g[l], k_norm_g[l], mlstm_i_b[l], mlstm_f_b[l], mlstm_norm_g[l],
                           a_log[l], dt_bias[l], conv_w[l], conv_b[l], d_skip[l])
        wa, wb, wc, wo = (w.astype(BF16) for w in (w_att_out[l], w_mlstm_out[l], w_ssm_out[l], w_out[l]))
        ys = [_layer(y, lp, ropes[y.shape[1]], bdk, norm_g[l], attn_sink[l], ssm_norm_g[l], wa, wb, wc, wo)
              for y in ys]
    return tuple(ys)
```

```python
import functools

import jax
import jax.numpy as jnp
import numpy as np
from jax import lax
from jax.experimental import pallas as pl
from jax.experimental.pallas import tpu as pltpu

F32 = jnp.float32
BF16 = jnp.bfloat16

D_MODEL = 1024
BRANCH = 512
RMS_EPS = 1e-6
CHUNK = 128
LANES = 128
SUBLANES = 8
BF16_ROWS = 16
ATT_HEADS = 8
ATT_HEAD_DIM = 64
ROT_HALF = 8
ROPE_THETA = 500000.0
MLSTM_HEADS = 4
MLSTM_DK = 128
MLSTM_BLOCK = 8
SSM_HEADS = 8
SSM_CONV = 5
HALO = BF16_ROWS
NEG = -1e30
LOG2E = float(np.log2(np.e))

MQK_W = 1024
XBC_W = 768
SMALL_W = 128
ROW_W = MQK_W + XBC_W + SMALL_W
T_MV, T_AQ, T_AV = 0, 512, 1024
T_ROWS = 1152
G_ROWS = 48
G_SSD = 32
W2_WIDTH = 5120
PROJ_TM = 1024
FINAL_TM = 512

VMEM_LIMIT = 56 * 1024 * 1024


def _dot(a, b):
    return jnp.dot(a, b, preferred_element_type=F32)


def _dot_nt(a, b):
    return lax.dot_general(a, b, (((1,), (1,)), ((), ())), preferred_element_type=F32)


def _split3(x):
    h1 = x.astype(BF16)
    r1 = x - h1.astype(F32)
    h2 = r1.astype(BF16)
    r2 = r1 - h2.astype(F32)
    return h1, h2, r2.astype(BF16)


def _cum_right(x, mask_bf):
    h1, h2, h3 = _split3(x)
    return _dot(h1, mask_bf) + _dot(h2, mask_bf) + _dot(h3, mask_bf)


def _sigmoid_of_twice(h):
    return 0.5 * jnp.tanh(h) + 0.5


def _silu_of_twice(h):
    return h * jnp.tanh(h) + h


def _softplus(x):
    return jnp.maximum(x, 0.0) + jnp.log1p(jnp.exp(-jnp.abs(x)))


def _log_sigmoid(x):
    return -_softplus(-x)


def _rms(x, g_row):
    ms = jnp.mean(x * x, axis=-1, keepdims=True)
    return x * lax.rsqrt(ms + RMS_EPS) * g_row


def _params(n_axes):
    return pltpu.CompilerParams(dimension_semantics=("arbitrary",) * n_axes, vmem_limit_bytes=VMEM_LIMIT)


def _proj_kernel(x_ref, ng_ref, wr_ref, wt_ref, wg_ref, mqk_ref, xbc_ref, small_ref, t_ref, g_ref):
    h = _rms(x_ref[...], ng_ref[...]).astype(BF16)
    for j0 in range(0, MQK_W, 512):
        mqk_ref[:, j0:j0 + 512] = _dot(h, wr_ref[:, j0:j0 + 512]).astype(BF16)
    xbc_ref[...] = _dot(h, wr_ref[:, MQK_W:MQK_W + XBC_W]).astype(BF16)
    small_ref[...] = _dot(h, wr_ref[:, MQK_W + XBC_W:])
    n_chunks = PROJ_TM // CHUNK
    for r0 in range(0, T_ROWS, 384):
        res = _dot_nt(wt_ref[r0:r0 + 384, :], h).astype(BF16)
        for j in range(n_chunks):
            t_ref[j, r0:r0 + 384, :] = res[:, j * CHUNK:(j + 1) * CHUNK]
    gates = _dot_nt(wg_ref[...], h)
    for j in range(n_chunks):
        g_ref[j] = gates[:, j * CHUNK:(j + 1) * CHUNK]


def _proj_call(x2, ng, w_row, w_t, w_g):
    t = x2.shape[0]
    tm = PROJ_TM
    full = lambda shape: pl.BlockSpec(shape, lambda i: (0, 0), pipeline_mode=pl.Buffered(1))
    rows = lambda width: pl.BlockSpec((tm, width), lambda i: (i, 0))
    chunks = lambda r: pl.BlockSpec((tm // CHUNK, r, CHUNK), lambda i: (i, 0, 0))
    return pl.pallas_call(
        _proj_kernel,
        grid=(t // tm,),
        in_specs=[rows(D_MODEL), full((1, D_MODEL)), full((D_MODEL, ROW_W)), full((T_ROWS, D_MODEL)),
                  full((G_ROWS, D_MODEL))],
        out_specs=[rows(MQK_W), rows(XBC_W), rows(SMALL_W), chunks(T_ROWS), chunks(G_ROWS)],
        out_shape=[
            jax.ShapeDtypeStruct((t, MQK_W), BF16),
            jax.ShapeDtypeStruct((t, XBC_W), BF16),
            jax.ShapeDtypeStruct((t, SMALL_W), F32),
            jax.ShapeDtypeStruct((t // CHUNK, T_ROWS, CHUNK), BF16),
            jax.ShapeDtypeStruct((t // CHUNK, G_ROWS, CHUNK), F32),
        ],
        compiler_params=_params(1),
        name="proj",
    )(x2, ng, w_row, w_t, w_g)


def _rope_rows(x, tab):
    w = x.shape[1]
    return (x * tab[:, :w]
            + pltpu.roll(x, w - ROT_HALF, 1) * tab[:, w:2 * w]
            + pltpu.roll(x, ROT_HALF, 1) * tab[:, 2 * w:])


ATT_QB = 8


def _att_kernel(kp_ref, kc_ref, kn_ref, qt_ref, vp_ref, vc_ref, vn_ref, rp_ref, rc_ref, rn_ref,
                rt_ref, qg_ref, kg_ref, bdk_ref, sink_ref, o_ref):
    i = pl.program_id(1)
    ni = pl.num_programs(1)
    win = 3 * CHUNK
    nkeys = (ATT_QB + 2) * CHUNK

    k = jnp.concatenate([kp_ref[...], kc_ref[...], kn_ref[...]], axis=0)
    tab_k = jnp.concatenate([rp_ref[...], rc_ref[...], rn_ref[...]], axis=0)
    kms = _dot((k * k).astype(BF16), bdk_ref[...])
    kr = _rope_rows(k * lax.rsqrt(kms + RMS_EPS) * kg_ref[...], tab_k)
    lane = lax.broadcasted_iota(jnp.int32, (nkeys, LANES), 1)
    lo = lane < ATT_HEAD_DIM
    k_sw = pltpu.roll(kr, ATT_HEAD_DIM, 1)
    zero = jnp.zeros_like(kr)
    k_var = [[jnp.where(lo, kr, zero).astype(BF16), jnp.where(lo, zero, k_sw).astype(BF16)],
             [jnp.where(lo, k_sw, zero).astype(BF16), jnp.where(lo, zero, kr).astype(BF16)]]

    v_t = jnp.concatenate([vp_ref[...]] + [vc_ref[j] for j in range(ATT_QB)] + [vn_ref[...]], axis=1)
    ones = jnp.ones((BF16_ROWS, win), BF16)

    s_idx = lax.broadcasted_iota(jnp.int32, (win, 2 * CHUNK), 0)
    t_idx = lax.broadcasted_iota(jnp.int32, (win, 2 * CHUNK), 1) % CHUNK
    band = (s_idx >= t_idx) & (s_idx <= t_idx + 2 * CHUNK)
    first_head = lax.broadcasted_iota(jnp.int32, (1, 2 * CHUNK), 1) < CHUNK

    s2 = []
    for qb in range(ATT_QB):
        cos = rt_ref[qb, 0:ROT_HALF, :]
        sin = rt_ref[qb, ROT_HALF:, :]
        q_heads = []
        for h in range(ATT_HEADS):
            x = qt_ref[qb, h * ATT_HEAD_DIM:(h + 1) * ATT_HEAD_DIM, :].astype(F32)
            ms = jnp.mean(x * x, axis=0, keepdims=True)
            xn = x * lax.rsqrt(ms + RMS_EPS) * qg_ref[...]
            x1 = xn[0:ROT_HALF]
            x2 = xn[ROT_HALF:2 * ROT_HALF]
            xr = jnp.concatenate([x1 * cos - x2 * sin, x2 * cos + x1 * sin, xn[2 * ROT_HALF:]], axis=0)
            q_heads.append((xr * (ATT_HEAD_DIM ** -0.5 * LOG2E)).astype(BF16))
        keys = slice(qb * CHUNK, qb * CHUNK + win)
        for g in range(2):
            lhs = jnp.concatenate([k_var[g][0][keys], k_var[g][1][keys]], axis=0)
            rhs = jnp.concatenate(
                [jnp.concatenate([q_heads[4 * g + 2 * j], q_heads[4 * g + 2 * j + 1]], axis=0)
                 for j in range(2)], axis=1)
            s2.append(_dot(lhs, rhs))
    ps, dens = [], []
    for qb in range(ATT_QB):
        valid = band
        if qb == 0:
            valid = valid & ((s_idx >= CHUNK) | (i > 0))
        if qb == ATT_QB - 1:
            valid = valid & ((s_idx < 2 * CHUNK) | (i < ni - 1))
        bias = jnp.where(valid, 0.0, NEG)
        for g in range(2):
            for half in range(2):
                s = s2[2 * qb + g][half * win:(half + 1) * win] + bias
                sink = jnp.where(first_head, sink_ref[4 * g + half], sink_ref[4 * g + 2 + half]) * LOG2E
                mx = jnp.maximum(jnp.max(s, axis=0, keepdims=True), sink)
                p = jnp.exp2(s - mx)
                dens.append(jnp.exp2(sink - mx))
                ps.append(p.astype(BF16))
    for qb in range(ATT_QB):
        outs = [None] * ATT_HEADS
        for g in range(2):
            vg = jnp.concatenate([v_t[g * ATT_HEAD_DIM:(g + 1) * ATT_HEAD_DIM, qb * CHUNK:qb * CHUNK + win], ones], axis=0)
            for half in range(2):
                n = 4 * qb + 2 * g + half
                od = _dot(vg, ps[n])
                o = od[:ATT_HEAD_DIM] * (1.0 / (od[ATT_HEAD_DIM:ATT_HEAD_DIM + 1] + dens[n]))
                outs[4 * g + half] = o[:, :CHUNK]
                outs[4 * g + 2 + half] = o[:, CHUNK:]
        o_ref[qb] = jnp.concatenate(outs, axis=0)


def _att_call(small, p_t, b, s, rope_rows, rope_t, qg_t, kg, bdk, sink):
    nb = s // CHUNK
    ni = nb // ATT_QB
    small3 = small.reshape(b, s, SMALL_W)
    prev = lambda i: jnp.maximum(ATT_QB * i - 1, 0)
    nxt = lambda i: jnp.minimum(ATT_QB * i + ATT_QB, nb - 1)
    av = T_AV // LANES
    return pl.pallas_call(
        _att_kernel,
        grid=(b, ni),
        in_specs=[
            pl.BlockSpec((None, CHUNK, LANES), lambda bi, i: (bi, prev(i), 0)),
            pl.BlockSpec((None, ATT_QB * CHUNK, LANES), lambda bi, i: (bi, i, 0)),
            pl.BlockSpec((None, CHUNK, LANES), lambda bi, i: (bi, nxt(i), 0)),
            pl.BlockSpec((ATT_QB, BRANCH, CHUNK), lambda bi, i: (bi * ni + i, T_AQ // BRANCH, 0)),
            pl.BlockSpec((None, LANES, CHUNK), lambda bi, i: (bi * nb + prev(i), av, 0)),
            pl.BlockSpec((ATT_QB, LANES, CHUNK), lambda bi, i: (bi * ni + i, av, 0)),
            pl.BlockSpec((None, LANES, CHUNK), lambda bi, i: (bi * nb + nxt(i), av, 0)),
            pl.BlockSpec((CHUNK, 3 * LANES), lambda bi, i: (prev(i), 0)),
            pl.BlockSpec((ATT_QB * CHUNK, 3 * LANES), lambda bi, i: (i, 0)),
            pl.BlockSpec((CHUNK, 3 * LANES), lambda bi, i: (nxt(i), 0)),
            pl.BlockSpec((ATT_QB, 2 * ROT_HALF, CHUNK), lambda bi, i: (i, 0, 0)),
            pl.BlockSpec((ATT_HEAD_DIM, LANES), lambda bi, i: (0, 0)),
            pl.BlockSpec((1, LANES), lambda bi, i: (0, 0)),
            pl.BlockSpec((LANES, LANES), lambda bi, i: (0, 0)),
            pl.BlockSpec(memory_space=pltpu.SMEM),
        ],
        out_specs=pl.BlockSpec((ATT_QB, BRANCH, CHUNK), lambda bi, i: (bi * ni + i, 0, 0)),
        out_shape=jax.ShapeDtypeStruct((b * nb, BRANCH, CHUNK), F32),
        compiler_params=_params(2),
        name="att",
    )(small3, small3, small3, p_t, p_t, p_t, p_t, rope_rows, rope_rows, rope_rows, rope_t,
      qg_t, kg, bdk, sink)


def _scan_masks(reverse):
    r = lax.broadcasted_iota(jnp.int32, (CHUNK, CHUNK), 0)
    c = lax.broadcasted_iota(jnp.int32, (CHUNK, CHUNK), 1)
    causal = (r <= c) if reverse else (r >= c)
    anti = (r >= c) if reverse else (r <= c)
    return causal, anti


def _scan_block(reverse, n):
    return (lambda c: n - 1 - c) if reverse else (lambda c: c)


AUG = MLSTM_DK + BF16_ROWS
LOG_QK_SCALE = float(np.log(MLSTM_DK ** -0.5))


def _mlstm_kernel(reverse, qk_ref, vt_ref, g_ref, dp_ref, *rest):
    if reverse:
        prev_ref, ng_ref, o_ref, st_sc, m_sc = rest
    else:
        o_ref, st_sc, m_sc = rest
    nblk = MLSTM_BLOCK
    rows = nblk * SUBLANES

    @pl.when(pl.program_id(1) == 0)
    def _():
        st_sc[...] = jnp.zeros_like(st_sc)
        m_sc[...] = jnp.zeros_like(m_sc)

    _, anti = _scan_masks(reverse)
    g = g_ref[...]
    ig = (g[:, 0:SUBLANES, :] + dp_ref[0:SUBLANES, :]).reshape(rows, CHUNK)
    lf = _log_sigmoid(g[:, SUBLANES:, :] + dp_ref[SUBLANES:, :]).reshape(rows, CHUNK)
    b = _cum_right(lf, anti.astype(BF16))
    a = ig - b
    a_max = jnp.max(a, axis=1, keepdims=True)
    b_tot = jnp.sum(lf, axis=1, keepdims=True)
    a_col = jnp.concatenate([a * LOG2E, jnp.zeros((CHUNK - rows, CHUNK), F32)], axis=0).T
    ones = jnp.ones((AUG - MLSTM_DK, CHUNK), BF16)
    scale = MLSTM_DK ** -0.5

    m_st = m_sc[...][:, 0:1]
    st = [st_sc[h] for h in range(MLSTM_HEADS)]
    order = range(nblk - 1, -1, -1) if reverse else range(nblk)
    for j in order:
        sl = slice(j * SUBLANES, (j + 1) * SUBLANES)
        tok = slice(j * CHUNK, (j + 1) * CHUNK)
        mm = jnp.maximum(m_st, a_max[sl])
        wk = jnp.exp(a[sl] - mm) * scale
        decay = jnp.exp(m_st - mm)
        zero_b = jnp.zeros((CHUNK, CHUNK), BF16)

        def block_diag(x0, x1):
            return jnp.concatenate([jnp.concatenate([x0, zero_b], axis=1),
                                    jnp.concatenate([zero_b, x1], axis=1)], axis=0)

        qs, ks, vts, sts = [], [], [], [None] * MLSTM_HEADS
        for h in range(MLSTM_HEADS):
            qs.append(qk_ref[tok, h * MLSTM_DK:(h + 1) * MLSTM_DK])
            ks.append(qk_ref[tok, BRANCH + h * MLSTM_DK:BRANCH + (h + 1) * MLSTM_DK])
            vts.append(jnp.concatenate([vt_ref[j, h * MLSTM_DK:(h + 1) * MLSTM_DK, :], ones], axis=0))
        for p in range(MLSTM_HEADS // 2):
            h0, h1 = 2 * p, 2 * p + 1
            lhs = jnp.concatenate([jnp.concatenate([ks[h], st[h].astype(BF16)], axis=0) for h in (h0, h1)], axis=1)
            both = _dot_nt(lhs, block_diag(qs[h0], qs[h1]))
            sts[h0] = both[:, :CHUNK]
            sts[h1] = both[:, CHUNK:]
        scs, w_inters, m_ts = [], [], []
        for h in range(MLSTM_HEADS):
            col = j * SUBLANES + h
            a_m = jnp.where(anti, jnp.broadcast_to(a_col[:, col:col + 1], (CHUNK, CHUNK)), NEG)
            m_h = m_st[h:h + 1, :]
            u2 = jnp.maximum(m_h * LOG2E, jnp.max(a_m, axis=0, keepdims=True))
            u = u2 * (1.0 / LOG2E)
            m_ts.append(b[col:col + 1, :] + u)
            w_inters.append(jnp.exp(m_h - u))
            scs.append((sts[h][:CHUNK] * jnp.exp2(a_m - (u2 - LOG_QK_SCALE * LOG2E))).astype(BF16))
        outs = [None] * MLSTM_HEADS
        for p in range(MLSTM_HEADS // 2):
            h0, h1 = 2 * p, 2 * p + 1
            nd2 = _dot(jnp.concatenate([vts[h0], vts[h1]], axis=1), block_diag(scs[h0], scs[h1]))
            for n, h in enumerate((h0, h1)):
                nd = nd2[:, n * CHUNK:(n + 1) * CHUNK] + w_inters[h] * sts[h][CHUNK:]
                den = jnp.maximum(jnp.abs(nd[MLSTM_DK:MLSTM_DK + 1, :]), jnp.exp(-m_ts[h]))
                outs[h] = nd[:MLSTM_DK] * (1.0 / den)
        for p in range(MLSTM_HEADS // 2):
            h0, h1 = 2 * p, 2 * p + 1
            vw = jnp.concatenate([(vts[h].astype(F32) * wk[h:h + 1, :]).astype(BF16) for h in (h0, h1)], axis=1)
            upd = _dot(vw, block_diag(ks[h0], ks[h1]))
            for n, h in enumerate((h0, h1)):
                st[h] = decay[h:h + 1, :] * st[h] + upd[:, n * CHUNK:(n + 1) * CHUNK]
        m_st = b_tot[sl] + mm
        if reverse:
            normed = []
            for h in range(MLSTM_HEADS):
                x = prev_ref[j, h * MLSTM_DK:(h + 1) * MLSTM_DK, :] + outs[h]
                ms = jnp.mean(x * x, axis=0, keepdims=True)
                normed.append(x * lax.rsqrt(ms + RMS_EPS))
            o_ref[j] = jnp.concatenate(normed, axis=0) * ng_ref[...]
        else:
            o_ref[j] = jnp.concatenate(outs, axis=0)
    for h in range(MLSTM_HEADS):
        st_sc[h] = st[h]
    m_sc[...] = jnp.broadcast_to(m_st, (SUBLANES, LANES))


def _mlstm_call(reverse, mqk, p_t, g_t, b, s, dirp, prev=None, ng_t=None):
    nb = s // (CHUNK * MLSTM_BLOCK)
    blk = _scan_block(reverse, nb)
    mqk3 = mqk.reshape(b, s, MQK_W)
    d = int(reverse)
    out_spec = pl.BlockSpec((MLSTM_BLOCK, BRANCH, CHUNK), lambda bi, c: (bi * nb + blk(c), 0, 0))
    in_specs = [
        pl.BlockSpec((None, CHUNK * MLSTM_BLOCK, MQK_W), lambda bi, c: (bi, blk(c), 0)),
        pl.BlockSpec((MLSTM_BLOCK, BRANCH, CHUNK), lambda bi, c: (bi * nb + blk(c), T_MV // BRANCH, 0)),
        pl.BlockSpec((MLSTM_BLOCK, 2 * SUBLANES, CHUNK), lambda bi, c: (bi * nb + blk(c), d, 0)),
        pl.BlockSpec((None, 2 * SUBLANES, LANES), lambda bi, c: (d, 0, 0)),
    ]
    args = [mqk3, p_t, g_t, dirp]
    if reverse:
        in_specs += [out_spec, pl.BlockSpec((BRANCH, LANES), lambda bi, c: (0, 0))]
        args += [prev, ng_t]
    return pl.pallas_call(
        functools.partial(_mlstm_kernel, reverse),
        grid=(b, nb),
        in_specs=in_specs,
        out_specs=out_spec,
        out_shape=jax.ShapeDtypeStruct((b * s // CHUNK, BRANCH, CHUNK), F32),
        scratch_shapes=[
            pltpu.VMEM((MLSTM_HEADS, AUG, MLSTM_DK), F32),
            pltpu.VMEM((SUBLANES, LANES), F32),
        ],
        compiler_params=_params(2),
        name="mlstm_bwd" if reverse else "mlstm_fwd",
    )(*args)


SSD_BLOCK = 4


def _ssd_gates(reverse, g_ref, dp_ref):
    _, anti = _scan_masks(reverse)
    pre = (g_ref[...] + dp_ref[0:SSM_HEADS, :]).reshape(SSD_BLOCK * SSM_HEADS, CHUNK)
    dt = _softplus(pre)
    a_rows = jnp.concatenate([-jnp.exp(dp_ref[SSM_HEADS:, :]) * LOG2E] * SSD_BLOCK, axis=0)
    da = dt * a_rows
    acs = _cum_right(da, anti.astype(BF16))
    acs_r = acs - jnp.log2(dt)
    tot = jnp.sum(da, axis=1, keepdims=True)
    n = SSD_BLOCK * SSM_HEADS
    acs_c = jnp.concatenate([acs, jnp.zeros((CHUNK - n, CHUNK), F32)], axis=0).T
    return acs_c, acs_r, tot


def _ssd_local(reverse, xs_b, bt, cm_b, gates, j_chunk):
    causal, _ = _scan_masks(reverse)
    acs_c, acs_r, tot = gates

    lane = lax.broadcasted_iota(jnp.int32, (CHUNK, LANES), 1)
    lo = lane < 64
    lo_row = lo[0:1, :]
    bt_b = bt.astype(BF16)
    zero_b = jnp.zeros((CHUNK, LANES), BF16)
    cb = [_dot(jnp.where(lo, cm_b, zero_b), bt_b), _dot(jnp.where(lo, zero_b, cm_b), bt_b)]

    out = []
    for j in range(4):
        grp = j // 2
        m_parts, bw_parts, eac, edec = [], [], [], []
        for e in (2 * j, 2 * j + 1):
            r = j_chunk * SSM_HEADS + e
            ac = acs_c[:, r:r + 1]
            ar = acs_r[r:r + 1, :]
            te = tot[r:r + 1, :]
            m_parts.append((cb[grp] * jnp.exp2(jnp.where(causal, ac - ar, NEG))).astype(BF16))
            bw_parts.append((bt * jnp.exp2(te - ar)).astype(BF16))
            eac.append(jnp.exp2(ac))
            edec.append(jnp.exp2(te))
        lhs = jnp.concatenate([jnp.concatenate(m_parts, axis=1),
                               jnp.concatenate(bw_parts, axis=1)], axis=0)
        xp = xs_b[:, j * LANES:(j + 1) * LANES]
        x2 = jnp.concatenate([jnp.where(lo, xp, zero_b), jnp.where(lo, zero_b, xp)], axis=0)
        out.append((_dot(lhs, x2), jnp.where(lo, eac[0], eac[1]), jnp.where(lo_row, edec[0], edec[1])))
    return out


def _ssd_carry(local, cm_b, st):
    row = lax.broadcasted_iota(jnp.int32, (CHUNK, LANES), 0)
    ys = []
    for j in range(4):
        res, eac, edec = local[j]
        grp = j // 2
        in_grp = (row >= 64 * grp) & (row < 64 * grp + 64)
        ys.append(res[:CHUNK] + _dot(cm_b, st[j].astype(BF16)) * eac)
        st[j] = st[j] * edec + jnp.where(in_grp, res[CHUNK:], 0.0)
    return jnp.concatenate(ys, axis=1)


def _ssd_fwd_kernel(x_ref, xp_ref, xn_ref, g_ref, dp_ref, cw_ref, cb_ref, dsk_ref, y_ref, xc_ref,
                    st_sc, pad_sc):
    c = pl.program_id(1)
    nc = pl.num_programs(1)
    rows = SSD_BLOCK * CHUNK

    @pl.when(c == 0)
    def _():
        st_sc[...] = jnp.zeros_like(st_sc)

    pad_sc[0:HALO, :] = jnp.where(c > 0, xp_ref[...].astype(F32), 0.0)
    pad_sc[HALO:HALO + rows, :] = x_ref[...].astype(F32)
    pad_sc[HALO + rows:, :] = jnp.where(c < nc - 1, xn_ref[...].astype(F32), 0.0)
    pad = pad_sc[...]
    n_pad = rows + 2 * HALO
    mid = SSM_CONV // 2
    shifted = [pad if t == mid else pltpu.roll(pad, (n_pad - (t - mid)) % n_pad, 0) for t in range(SSM_CONV)]
    gates = _ssd_gates(False, g_ref, dp_ref)
    xcs, local = [], []
    for j in range(SSD_BLOCK):
        tok = slice(j * CHUNK, (j + 1) * CHUNK)
        acc = jnp.broadcast_to(cb_ref[...], (CHUNK, XBC_W))
        for t in range(SSM_CONV):
            acc = acc + shifted[t][HALO + j * CHUNK:HALO + (j + 1) * CHUNK, :] * cw_ref[t:t + 1, :]
        xc = _silu_of_twice(acc)
        xc_b = xc.astype(BF16)
        xc_ref[tok, :] = xc_b
        xcs.append((xc, xc_b))
        local.append(_ssd_local(False, xc_b[:, :BRANCH], xc[:, BRANCH:BRANCH + LANES].T,
                                xc_b[:, BRANCH + LANES:], gates, j))
    st = [st_sc[j] for j in range(4)]
    for j in range(SSD_BLOCK):
        xc, xc_b = xcs[j]
        y = _ssd_carry(local[j], xc_b[:, BRANCH + LANES:], st)
        y_ref[j * CHUNK:(j + 1) * CHUNK, :] = y + dsk_ref[...] * xc[:, :BRANCH]
    for j in range(4):
        st_sc[j] = st[j]


def _ssd_bwd_kernel(xc_ref, g_ref, dp_ref, prev_ref, y_ref, st_sc):
    @pl.when(pl.program_id(1) == 0)
    def _():
        st_sc[...] = jnp.zeros_like(st_sc)

    order = range(SSD_BLOCK - 1, -1, -1)
    gates = _ssd_gates(True, g_ref, dp_ref)
    local = {}
    for j in order:
        tok = slice(j * CHUNK, (j + 1) * CHUNK)
        bt = xc_ref[tok, BRANCH:BRANCH + LANES].astype(F32).T
        local[j] = _ssd_local(True, xc_ref[tok, :BRANCH], bt, xc_ref[tok, BRANCH + LANES:], gates, j)
    st = [st_sc[j] for j in range(4)]
    for j in order:
        tok = slice(j * CHUNK, (j + 1) * CHUNK)
        y_ref[tok, :] = prev_ref[tok, :] + _ssd_carry(local[j], xc_ref[tok, BRANCH + LANES:], st)
    for j in range(4):
        st_sc[j] = st[j]


def _ssd_calls(xbc, g_t, b, s, dirp, conv_w8, conv_b, dskip):
    rows = SSD_BLOCK * CHUNK
    nb = s // rows
    per = rows // HALO
    last = s // HALO - 1
    xbc3 = xbc.reshape(b, s, XBC_W)
    state = pltpu.VMEM((4, CHUNK, LANES), F32)
    y_shape = jax.ShapeDtypeStruct((b, s, BRANCH), F32)
    const = lambda shape: pl.BlockSpec(shape, lambda bi, c: (0,) * len(shape))

    def block_spec(width, blk, col=0):
        return pl.BlockSpec((None, rows, width), lambda bi, c: (bi, blk(c), col))

    fwd = _scan_block(False, nb)
    y_fwd, xc = pl.pallas_call(
        _ssd_fwd_kernel,
        grid=(b, nb),
        in_specs=[
            block_spec(XBC_W, fwd),
            pl.BlockSpec((None, HALO, XBC_W), lambda bi, c: (bi, jnp.maximum(c * per - 1, 0), 0)),
            pl.BlockSpec((None, HALO, XBC_W), lambda bi, c: (bi, jnp.minimum(c * per + per, last), 0)),
            pl.BlockSpec((SSD_BLOCK, SSM_HEADS, CHUNK), lambda bi, c: (bi * nb + fwd(c), G_SSD // SSM_HEADS, 0)),
            pl.BlockSpec((None, 2 * SSM_HEADS, LANES), lambda bi, c: (0, 0, 0)),
            const((SUBLANES, XBC_W)), const((1, XBC_W)), const((1, BRANCH)),
        ],
        out_specs=[block_spec(BRANCH, fwd), block_spec(XBC_W, fwd)],
        out_shape=[y_shape, jax.ShapeDtypeStruct((b, s, XBC_W), BF16)],
        scratch_shapes=[state, pltpu.VMEM((rows + 2 * HALO, XBC_W), F32)],
        compiler_params=_params(2),
        name="ssd_fwd",
    )(xbc3, xbc3, xbc3, g_t, dirp, conv_w8, conv_b, dskip)

    bwd = _scan_block(True, nb)
    return pl.pallas_call(
        _ssd_bwd_kernel,
        grid=(b, nb),
        in_specs=[
            block_spec(XBC_W, bwd),
            pl.BlockSpec((SSD_BLOCK, SSM_HEADS, CHUNK), lambda bi, c: (bi * nb + bwd(c), G_SSD // SSM_HEADS + 1, 0)),
            pl.BlockSpec((None, 2 * SSM_HEADS, LANES), lambda bi, c: (1, 0, 0)),
            block_spec(BRANCH, bwd),
        ],
        out_specs=block_spec(BRANCH, bwd),
        out_shape=y_shape,
        scratch_shapes=[state],
        compiler_params=_params(2),
        name="ssd_bwd",
    )(xc, g_t, dirp, y_fwd)


def _final_kernel(x_ref, yat_ref, hmt_ref, ys_ref, ng_ref, sg_ref, w2_ref, wa_ref, wb_ref, wc_ref,
                  wo_ref, o_ref):
    x = x_ref[...]
    h = _rms(x, ng_ref[...]).astype(BF16)

    def proj(k):
        return _dot(h, w2_ref[:, k * BRANCH:(k + 1) * BRANCH])

    def gate(k):
        lo = 4 * BRANCH + k * D_MODEL
        return _sigmoid_of_twice(_dot(h, w2_ref[:, lo:lo + D_MODEL]))

    def token_major(ref):
        return jnp.concatenate([ref[j].T for j in range(FINAL_TM // CHUNK)], axis=0)

    ya = token_major(yat_ref) * _silu_of_twice(proj(0))
    merged = gate(0) * _dot(ya.astype(BF16), wa_ref[...])
    yb = token_major(hmt_ref) * _sigmoid_of_twice(proj(1)) * _silu_of_twice(proj(2))
    merged = merged + gate(1) * _dot(yb.astype(BF16), wb_ref[...])
    yc = _rms(ys_ref[...] * _silu_of_twice(proj(3)), sg_ref[...])
    merged = merged + gate(2) * _dot(yc.astype(BF16), wc_ref[...])
    o_ref[...] = x + _dot(merged.astype(BF16), wo_ref[...])


def _final_call(x2, ya_t, hm_t, ys, ng, sg, w2, wa, wb, wc, wo):
    t = x2.shape[0]
    tm = FINAL_TM
    row = lambda width: pl.BlockSpec((tm, width), lambda i: (i, 0))
    chunked = pl.BlockSpec((tm // CHUNK, BRANCH, CHUNK), lambda i: (i, 0, 0))
    full = lambda shape: pl.BlockSpec(shape, lambda i: (0, 0), pipeline_mode=pl.Buffered(1))
    return pl.pallas_call(
        _final_kernel,
        grid=(t // tm,),
        in_specs=[
            row(D_MODEL), chunked, chunked, row(BRANCH),
            full((1, D_MODEL)), full((1, BRANCH)),
            full((D_MODEL, W2_WIDTH)), full((BRANCH, D_MODEL)), full((BRANCH, D_MODEL)),
            full((BRANCH, D_MODEL)), full((D_MODEL, D_MODEL)),
        ],
        out_specs=row(D_MODEL),
        out_shape=jax.ShapeDtypeStruct((t, D_MODEL), F32),
        compiler_params=_params(1),
        name="final",
    )(x2, ya_t, hm_t, ys, ng, sg, w2, wa, wb, wc, wo)


_IN_SPLITS = (512, 128, 128, 512, 512, 512, 512, 512, 8, 8, 512, 768, 16, 512, 3072)
_NAMES = ("aq", "ak", "av", "az", "mq", "mk", "mv", "mo", "mi", "mf", "mz", "sxbc", "sdt", "sz", "gates")


def _split_w_in(w):
    bounds = np.cumsum((0,) + _IN_SPLITS)
    return {n: w[:, bounds[i]:bounds[i + 1]] for i, n in enumerate(_NAMES)}


def _layer_params(w_in, q_norm_g, k_norm_g, mlstm_i_b, mlstm_f_b, mlstm_norm_g, a_log, dt_bias, conv_w,
                  conv_b, d_skip):
    p = _split_w_in(w_in)
    zc = lambda n: jnp.zeros((D_MODEL, n), F32)
    w_row = jnp.concatenate([p["mq"], p["mk"], p["sxbc"], p["ak"]], axis=1).astype(BF16)
    w_t = jnp.concatenate([p["mv"], p["aq"], p["av"]], axis=1).T.astype(BF16)
    m_gates = []
    for d in range(2):
        m_gates += [p["mi"][:, 4 * d:4 * d + 4], zc(4), p["mf"][:, 4 * d:4 * d + 4], zc(4)]
    w_g = jnp.concatenate(m_gates + [p["sdt"]], axis=1).T.astype(BF16)
    w2 = (0.5 * jnp.concatenate([p["az"], p["mo"], p["mz"], p["sz"], p["gates"]], axis=1)).astype(BF16)
    sdirp = jnp.broadcast_to(jnp.concatenate([dt_bias, a_log], axis=1)[:, :, None], (2, 2 * SSM_HEADS, LANES))
    mdirp = jnp.zeros((2, 2 * SUBLANES, LANES), F32)
    mdirp = mdirp.at[:, 0:4, :].set(mlstm_i_b[:, :, None]).at[:, 8:12, :].set(mlstm_f_b[:, :, None])
    return dict(
        w_row=w_row, w_t=w_t, w_g=w_g, w2=w2, sdirp=sdirp, mdirp=mdirp,
        qg_t=jnp.broadcast_to(q_norm_g[:, None], (ATT_HEAD_DIM, LANES)),
        kg=jnp.tile(k_norm_g, 2)[None, :],
        ng_t=jnp.broadcast_to(mlstm_norm_g[:, None], (BRANCH, LANES)),
        conv_w8=jnp.zeros((SUBLANES, XBC_W), F32).at[:SSM_CONV].set(0.5 * conv_w), conv_b=0.5 * conv_b[None, :],
        dskip=jnp.repeat(d_skip, 64)[None, :])


def _rope_tables(s_len):
    inv_freq = ROPE_THETA ** (-jnp.arange(ROT_HALF, dtype=F32) * 2.0 / (2 * ROT_HALF))
    ang = jnp.arange(s_len).astype(F32)[:, None] * inv_freq
    cos, sin = jnp.cos(ang), jnp.sin(ang)
    rest = ATT_HEAD_DIM - 2 * ROT_HALF
    one = jnp.ones((s_len, rest), F32)
    zer = jnp.zeros((s_len, rest), F32)
    z8 = jnp.zeros((s_len, ROT_HALF), F32)
    tc = jnp.concatenate([cos, cos, one], axis=1)
    t1 = jnp.concatenate([-sin, z8, zer], axis=1)
    t2 = jnp.concatenate([z8, sin, zer], axis=1)
    rows = jnp.concatenate([jnp.tile(t, (1, 2)) for t in (tc, t1, t2)], axis=1)
    nb = s_len // CHUNK
    t_tab = jnp.concatenate([cos.T.reshape(ROT_HALF, nb, CHUNK), sin.T.reshape(ROT_HALF, nb, CHUNK)], axis=0)
    return rows, t_tab.transpose(1, 0, 2)


def _block_diag_mean(width):
    blk = np.kron(np.eye(width // ATT_HEAD_DIM), np.full((ATT_HEAD_DIM, ATT_HEAD_DIM), 1.0 / ATT_HEAD_DIM))
    return jnp.asarray(blk, BF16)


def _layer(x, lp, ropes, bdk, norm_g, attn_sink, ssm_norm_g, wa, wb, wc, wo):
    b, s, _ = x.shape
    assert s % (CHUNK * max(MLSTM_BLOCK, SSD_BLOCK, ATT_QB)) == 0 and (b * s) % max(PROJ_TM, FINAL_TM) == 0, (b, s)
    x2 = x.reshape(b * s, D_MODEL)
    mqk, xbc, small, p_t, g_t = _proj_call(x2, norm_g[None, :], lp["w_row"], lp["w_t"], lp["w_g"])
    ya_t = _att_call(small, p_t, b, s, ropes[0], ropes[1], lp["qg_t"], lp["kg"], bdk, attn_sink)
    hm_fwd = _mlstm_call(False, mqk, p_t, g_t, b, s, lp["mdirp"])
    hm_t = _mlstm_call(True, mqk, p_t, g_t, b, s, lp["mdirp"], hm_fwd, lp["ng_t"])
    ys = _ssd_calls(xbc, g_t, b, s, lp["sdirp"], lp["conv_w8"], lp["conv_b"], lp["dskip"])
    out = _final_call(x2, ya_t, hm_t, ys.reshape(b * s, BRANCH), norm_g[None, :], ssm_norm_g[None, :],
                      lp["w2"], wa, wb, wc, wo)
    return out.reshape(b, s, D_MODEL)


def kernel(x_prompt, x_sample, norm_g, w_in, q_norm_g, k_norm_g, attn_sink, w_att_out, mlstm_i_b, mlstm_f_b,
           mlstm_norm_g, w_mlstm_out, conv_w, conv_b, a_log, dt_bias, d_skip, ssm_norm_g, w_ssm_out, w_out):
    depth = w_in.shape[0]
    bdk = _block_diag_mean(LANES)
    ropes = {x.shape[1]: _rope_tables(x.shape[1]) for x in (x_prompt, x_sample)}
    ys = [x_prompt, x_sample]
    for l in range(depth):
        lp = _layer_params(w_in[l], q_norm_g[l], k_norm_g[l], mlstm_i_b[l], mlstm_f_b[l], mlstm_norm_g[l],
                           a_log[l], dt_bias[l], conv_w[l], conv_b[l], d_skip[l])
        wa, wb, wc, wo = (w.astype(BF16) for w in (w_att_out[l], w_mlstm_out[l], w_ssm_out[l], w_out[l]))
        ys = [_layer(y, lp, ropes[y.shape[1]], bdk, norm_g[l], attn_sink[l], ssm_norm_g[l], wa, wb, wc, wo)
              for y in ys]
    return tuple(ys)
```
